```python
import math
import jax, jax.numpy as jnp
from jax import lax
import numpy as np

D_MODEL = 1024
BATCH = 2
SEQ = 8192
DEPTH = 4

CHUNK = 64
N_EVEN = (DEPTH + 1) // 2
N_ODD = DEPTH // 2
CONV_WIDTH = 3
CONV_DIM = D_MODEL // 2
POOL_DIM = D_MODEL // 2
POOL_WINDOWS = (2, 4, 8, 16)
POOL_GROUP = POOL_DIM // len(POOL_WINDOWS)
SSM_DIM = D_MODEL // 2
SSM_GROUP = 16
SSM_GROUPS = SSM_DIM // SSM_GROUP
SSM_STATE = 64
SGU_DIM = D_MODEL // 2
SGU_HEADS = 4
SGU_HEAD_DIM = SGU_DIM // SGU_HEADS
SGU_LEN = 128
EVEN_IN = 3 * CONV_DIM + POOL_DIM
ODD_IN = SSM_DIM + 2 * SGU_DIM
MIX_OUT = CONV_DIM + POOL_DIM
FFN_DIM = 2816
N_EXPERTS = 8
TOP_K = 2
EXPERT_DIM = 3584
RMS_EPS = 1e-5
LN_EPS = 1e-5
DT_MIN = 1e-3
DT_MAX = 1e-1

kernel_name = 'hybrid_conv_pool_s5_gmlp_moe_trunk'


def rms_norm(x, g):
    xf = x.astype(jnp.float32)
    y = xf * lax.rsqrt(jnp.mean(xf * xf, axis=-1, keepdims=True) + RMS_EPS)
    return (y * g.astype(jnp.float32)).astype(x.dtype)


def swiglu(x, w1, w3, w2):
    return (jax.nn.silu(x @ w1) * (x @ w3)) @ w2


def short_conv_mixer(gb, gc, xa, conv_w):
    z = gc * xa
    l = z.shape[1]
    zp = jnp.pad(z, ((0, 0), (CONV_WIDTH - 1, 0), (0, 0)))
    y = conv_w[0] * zp[:, 0:l]
    for k in range(1, CONV_WIDTH):
        y = y + conv_w[k] * zp[:, k:k + l]
    return gb * y


def pool_mixer(p, pool_w, pool_scale):
    b, l, _ = p.shape
    pg = p.astype(jnp.float32).reshape(b, l, len(POOL_WINDOWS), POOL_GROUP)
    cs = jnp.cumsum(pg, axis=1)
    pos = jnp.arange(1, l + 1, dtype=jnp.float32)
    outs = []
    for gi, w in enumerate(POOL_WINDOWS):
        c = cs[:, :, gi]
        c_prev = jnp.pad(c, ((0, 0), (w, 0), (0, 0)))[:, :l]
        cnt = jnp.minimum(pos, float(w))[None, :, None]
        outs.append((c - c_prev) / cnt - pg[:, :, gi])
    pooled = jnp.stack(outs, axis=2)
    mixed = jnp.einsum('blgc,gcd->blgd', pooled, pool_w.astype(jnp.float32))
    out = mixed.reshape(b, l, POOL_DIM) * pool_scale.astype(jnp.float32)
    return out.astype(p.dtype)


def _complex_scan_combine(e1, e2):
    a1r, a1i, b1r, b1i = e1
    a2r, a2i, b2r, b2i = e2
    ar = a2r * a1r - a2i * a1i
    ai = a2r * a1i + a2i * a1r
    br = a2r * b1r - a2i * b1i + b2r
    bi = a2r * b1i + a2i * b1r + b2i
    return (ar, ai, br, bi)


def s5_mixer(u, lam_re, lam_im, log_dt, b_re, b_im, c_re, c_im, d_skip, glu_w, glu_b):
    f32 = jnp.float32
    bsz, l, _ = u.shape
    uf = u.astype(f32)
    ug = uf.reshape(bsz, l, SSM_GROUPS, SSM_GROUP)
    lr = lam_re.astype(f32)
    li = lam_im.astype(f32)
    dt = jnp.exp(log_dt.astype(f32))[:, None]
    mag = jnp.exp(lr * dt)
    abar_r = mag * jnp.cos(li * dt)
    abar_i = mag * jnp.sin(li * dt)
    qr = abar_r - 1.0
    qi = abar_i
    den = lr * lr + li * li
    fr = ((qr * lr + qi * li) / den)[..., None]
    fi = ((qi * lr - qr * li) / den)[..., None]
    br = b_re.astype(f32)
    bi = b_im.astype(f32)
    bbar_r = fr * br - fi * bi
    bbar_i = fr * bi + fi * br
    bu_r = jnp.einsum('blgh,gph->blgp', ug, bbar_r)
    bu_i = jnp.einsum('blgh,gph->blgp', ug, bbar_i)
    a_r = jnp.broadcast_to(abar_r, bu_r.shape)
    a_i = jnp.broadcast_to(abar_i, bu_i.shape)
    _, _, h_r, h_i = lax.associative_scan(_complex_scan_combine, (a_r, a_i, bu_r, bu_i), axis=1)
    y = (jnp.einsum('ghp,blgp->blgh', c_re.astype(f32), h_r)
         - jnp.einsum('ghp,blgp->blgh', c_im.astype(f32), h_i))
    y = y.reshape(bsz, l, SSM_DIM) + d_skip.astype(f32) * uf
    g = jax.nn.gelu(y)
    out = g * jax.nn.sigmoid(g @ glu_w.astype(f32) + glu_b.astype(f32))
    return out.astype(u.dtype)


def sgu_mixer(zu, zv, ln_g, sgu_w, sgu_b):
    f32 = jnp.float32
    bsz, l, _ = zu.shape
    u = jax.nn.gelu(zu)
    v = jax.nn.gelu(zv).astype(f32).reshape(bsz, l, SGU_HEADS, SGU_HEAD_DIM)
    mu = jnp.mean(v, axis=-1, keepdims=True)
    var = jnp.mean(jnp.square(v - mu), axis=-1, keepdims=True)
    v = (v - mu) * lax.rsqrt(var + LN_EPS) * ln_g.astype(f32).reshape(SGU_HEADS, SGU_HEAD_DIM)
    v = v.reshape(bsz, l // SGU_LEN, SGU_LEN, SGU_HEADS, SGU_HEAD_DIM)
    cidx = jnp.arange(SGU_LEN) // CHUNK
    mask = cidx[None, :] <= cidx[:, None]
    w = jnp.where(mask[None], sgu_w.astype(f32), 0.0)
    s = jnp.einsum('hij,bnjhc->bnihc', w, v) + sgu_b.astype(f32).T[None, None, :, :, None]
    return u * s.reshape(bsz, l, SGU_DIM).astype(u.dtype)


def moe_swiglu(h, router_w, w1, w3, w2):
    bsz, l, d = h.shape
    xf = h.reshape(-1, d)
    logits = (xf @ router_w).astype(jnp.float32)
    top_v, top_i = lax.top_k(logits, TOP_K)
    top_w = jax.nn.softmax(top_v, axis=-1)
    gates = jnp.sum(jax.nn.one_hot(top_i, N_EXPERTS, dtype=jnp.float32) * top_w[..., None], axis=1)
    gates = gates.astype(h.dtype)
    out = jnp.zeros_like(xf)
    for e in range(N_EXPERTS):
        out = out + gates[:, e:e + 1] * swiglu(xf, w1[e], w3[e], w2[e])
    return out.reshape(bsz, l, d)


def setup_inputs(seed: int = 0) -> dict:
    key = jax.random.key(seed)
    k = jax.random.split(key, 40)
    f32 = jnp.float32
    D = D_MODEL
    ne, no = N_EVEN, N_ODD

    def nrm(kk, shape, scale):
        return jax.random.normal(kk, shape, f32) * scale

    def gain(kk, shape):
        return 1.0 + nrm(kk, shape, 0.02)

    n_idx = jnp.arange(SSM_STATE, dtype=f32)
    return {
        'x': nrm(k[0], (BATCH, SEQ, D), 1.0),
        'ev_norm1_g': gain(k[1], (ne, D)),
        'ev_w_in': nrm(k[2], (ne, D, EVEN_IN), D ** -0.5),
        'ev_conv_w': nrm(k[3], (ne, CONV_WIDTH, CONV_DIM), CONV_WIDTH ** -0.5),
        'ev_pool_w': nrm(k[4], (ne, len(POOL_WINDOWS), POOL_GROUP, POOL_GROUP), POOL_GROUP ** -0.5),
        'ev_pool_scale': 1.0 + nrm(k[5], (ne, POOL_DIM), 0.1),
        'ev_w_out': nrm(k[6], (ne, MIX_OUT, D), MIX_OUT ** -0.5),
        'ev_norm2_g': gain(k[7], (ne, D)),
        'ev_ffn_w1': nrm(k[8], (ne, D, FFN_DIM), D ** -0.5),
        'ev_ffn_w3': nrm(k[9], (ne, D, FFN_DIM), D ** -0.5),
        'ev_ffn_w2': nrm(k[10], (ne, FFN_DIM, D), FFN_DIM ** -0.5),
        'od_norm1_g': gain(k[11], (no, D)),
        'od_w_in': nrm(k[12], (no, D, ODD_IN), D ** -0.5),
        'od_lambda_re': -0.5 * jnp.exp(nrm(k[13], (no, SSM_GROUPS, SSM_STATE), 0.02)),
        'od_lambda_im': math.pi * n_idx + nrm(k[14], (no, SSM_GROUPS, SSM_STATE), 0.01),
        'od_log_dt': jax.random.uniform(k[15], (no, SSM_GROUPS), f32, math.log(DT_MIN), math.log(DT_MAX)),
        'od_b_re': nrm(k[16], (no, SSM_GROUPS, SSM_STATE, SSM_GROUP), (2 * SSM_GROUP) ** -0.5),
        'od_b_im': nrm(k[17], (no, SSM_GROUPS, SSM_STATE, SSM_GROUP), (2 * SSM_GROUP) ** -0.5),
        'od_c_re': nrm(k[18], (no, SSM_GROUPS, SSM_GROUP, SSM_STATE), SSM_STATE ** -0.5),
        'od_c_im': nrm(k[19], (no, SSM_GROUPS, SSM_GROUP, SSM_STATE), SSM_STATE ** -0.5),
        'od_d': nrm(k[20], (no, SSM_DIM), 1.0),
        'od_glu_w': nrm(k[21], (no, SSM_DIM, SSM_DIM), SSM_DIM ** -0.5),
        'od_glu_b': nrm(k[22], (no, SSM_DIM), 0.01),
        'od_sgu_ln_g': gain(k[23], (no, SGU_DIM)),
        'od_sgu_w': nrm(k[24], (no, SGU_HEADS, SGU_LEN, SGU_LEN), SGU_LEN ** -0.5),
        'od_sgu_b': 1.0 + nrm(k[25], (no, SGU_HEADS, SGU_LEN), 0.01),
        'od_w_out': nrm(k[26], (no, MIX_OUT, D), MIX_OUT ** -0.5),
        'od_norm2_g': gain(k[27], (no, D)),
        'od_router_w': nrm(k[28], (no, D, N_EXPERTS), D ** -0.5),
        'od_moe_w1': nrm(k[29], (no, N_EXPERTS, D, EXPERT_DIM), D ** -0.5),
        'od_moe_w3': nrm(k[30], (no, N_EXPERTS, D, EXPERT_DIM), D ** -0.5),
        'od_moe_w2': nrm(k[31], (no, N_EXPERTS, EXPERT_DIM, D), EXPERT_DIM ** -0.5),
        'final_norm_g': gain(k[32], (D,)),
    }


def reference(x, ev_norm1_g, ev_w_in, ev_conv_w, ev_pool_w, ev_pool_scale, ev_w_out,
              ev_norm2_g, ev_ffn_w1, ev_ffn_w3, ev_ffn_w2,
              od_norm1_g, od_w_in, od_lambda_re, od_lambda_im, od_log_dt,
              od_b_re, od_b_im, od_c_re, od_c_im, od_d, od_glu_w, od_glu_b,
              od_sgu_ln_g, od_sgu_w, od_sgu_b, od_w_out, od_norm2_g,
              od_router_w, od_moe_w1, od_moe_w3, od_moe_w2, final_norm_g):
    h = x
    for layer in range(DEPTH):
        i = layer // 2
        if layer % 2 == 0:
            z = rms_norm(h, ev_norm1_g[i]) @ ev_w_in[i]
            gb, gc, xa, p = jnp.split(z, [CONV_DIM, 2 * CONV_DIM, 3 * CONV_DIM], axis=-1)
            mix = jnp.concatenate([short_conv_mixer(gb, gc, xa, ev_conv_w[i]),
                                   pool_mixer(p, ev_pool_w[i], ev_pool_scale[i])], axis=-1)
            h = h + mix @ ev_w_out[i]
            h = h + swiglu(rms_norm(h, ev_norm2_g[i]), ev_ffn_w1[i], ev_ffn_w3[i], ev_ffn_w2[i])
        else:
            z = rms_norm(h, od_norm1_g[i]) @ od_w_in[i]
            u, zu, zv = jnp.split(z, [SSM_DIM, SSM_DIM + SGU_DIM], axis=-1)
            ssm = s5_mixer(u, od_lambda_re[i], od_lambda_im[i], od_log_dt[i], od_b_re[i], od_b_im[i],
                           od_c_re[i], od_c_im[i], od_d[i], od_glu_w[i], od_glu_b[i])
            sgu = sgu_mixer(zu, zv, od_sgu_ln_g[i], od_sgu_w[i], od_sgu_b[i])
            mix = jnp.concatenate([ssm, sgu], axis=-1)
            h = h + mix @ od_w_out[i]
            h = h + moe_swiglu(rms_norm(h, od_norm2_g[i]), od_router_w[i], od_moe_w1[i],
                               od_moe_w3[i], od_moe_w2[i])
    return rms_norm(h, final_norm_g)
```

```python
import functools
import math

import jax
import jax.numpy as jnp
from jax import lax
from jax.experimental import pallas as pl
from jax.experimental.pallas import tpu as pltpu

F32 = jnp.float32
BF16 = jnp.bfloat16

RMS_EPS = 1e-5
LN_EPS = 1e-5
CHUNK = 64
POOL_WINDOWS = (2, 4, 8, 16)
CONV_WIDTH = 3
TOP_K = 2

LANES = 128
V7X_VMEM_BYTES = 64 * 1024 * 1024
VMEM_LIMIT = V7X_VMEM_BYTES * 7 // 8
HALO = 16
T_SSM = LANES
TOKEN_TILE = 512
EXPERT_TILE = 512
EXPERT_F_TILE = 1792
FFN_F_TILE = 1024


def _const_spec(shape):
    nd = len(shape)
    return pl.BlockSpec(shape, lambda *_: (0,) * nd, pipeline_mode=pl.Buffered(1))


def _params(*sem):
    return pltpu.CompilerParams(dimension_semantics=sem, vmem_limit_bytes=VMEM_LIMIT)


def _rms(x, g):
    return x * lax.rsqrt(jnp.mean(x * x, axis=-1, keepdims=True) + RMS_EPS) * g


def _dot(a, b):
    return jnp.dot(a, b, preferred_element_type=F32)


def _even_mixer_kernel(h_ref, g_ref, win_ref, cw_ref, pw_ref, ps_ref, wout_ref, o_ref,
                       zc_ext, p_ext, *, tm, cdim):
    j = pl.program_id(1)

    @pl.when(j == 0)
    def _():
        zc_ext[0:HALO, :] = jnp.zeros((HALO, cdim), F32)
        p_ext[0:HALO, :] = jnp.zeros((HALO, cdim), F32)

    h = h_ref[...]
    hn = _rms(h, g_ref[...]).astype(BF16)
    z = _dot(hn, win_ref[...])
    gb = z[:, 0:cdim]
    zc = z[:, cdim:2 * cdim] * z[:, 2 * cdim:3 * cdim]
    p = z[:, 3 * cdim:4 * cdim]
    zc_ext[HALO:HALO + tm, :] = zc
    p_ext[HALO:HALO + tm, :] = p

    y = cw_ref[CONV_WIDTH - 1:CONV_WIDTH, :] * zc
    for k in range(CONV_WIDTH - 1):
        sh = CONV_WIDTH - 1 - k
        y = y + cw_ref[k:k + 1, :] * zc_ext[HALO - sh:HALO - sh + tm, :]
    parts = [(gb * y).astype(BF16)]

    pos = (j * tm + 1 + lax.broadcasted_iota(jnp.int32, (tm, LANES), 0)).astype(F32)
    for gi, w in enumerate(POOL_WINDOWS):
        c0 = gi * LANES
        pg = p[:, c0:c0 + LANES]
        s = pg
        for k in range(1, w):
            s = s + p_ext[HALO - k:HALO - k + tm, c0:c0 + LANES]
        pooled = s / jnp.minimum(pos, float(w)) - pg
        mixed = _dot(pooled.astype(BF16), pw_ref[gi]) * ps_ref[:, c0:c0 + LANES]
        parts.append(mixed.astype(BF16))

    mix = jnp.concatenate(parts, axis=1)
    o_ref[...] = h + _dot(mix, wout_ref[...])

    zc_ext[0:HALO, :] = zc_ext[tm:tm + HALO, :]
    p_ext[0:HALO, :] = p_ext[tm:tm + HALO, :]


def _even_mixer(h, g, w_in, conv_w, pool_w, pool_scale, w_out, *, bsz, seq):
    n, d = h.shape
    cdim = w_in.shape[1] // 4
    tm = min(TOKEN_TILE, seq)
    nj = seq // tm
    kern = functools.partial(_even_mixer_kernel, tm=tm, cdim=cdim)
    return pl.pallas_call(
        kern,
        out_shape=jax.ShapeDtypeStruct((n, d), F32),
        grid=(bsz, nj),
        in_specs=[
            pl.BlockSpec((tm, d), lambda b, j: (b * nj + j, 0)),
            _const_spec((1, d)),
            _const_spec(w_in.shape),
            _const_spec(conv_w.shape),
            _const_spec(pool_w.shape),
            _const_spec((1, cdim)),
            _const_spec(w_out.shape),
        ],
        out_specs=pl.BlockSpec((tm, d), lambda b, j: (b * nj + j, 0)),
        scratch_shapes=[pltpu.VMEM((HALO + tm, cdim), F32), pltpu.VMEM((HALO + tm, cdim), F32)],
        compiler_params=_params("arbitrary", "arbitrary"),
        name="even_mixer",
    )(h, g, w_in, conv_w, pool_w, pool_scale, w_out)


def _ffn_kernel(h_ref, g_ref, w1_ref, w3_ref, w2_ref, o_ref, *, fdim):
    h = h_ref[...]
    hn = _rms(h, g_ref[...]).astype(BF16)
    acc = h
    for f0 in range(0, fdim, FFN_F_TILE):
        f1 = min(f0 + FFN_F_TILE, fdim)
        a = _dot(hn, w1_ref[:, f0:f1])
        b = _dot(hn, w3_ref[:, f0:f1])
        act = (a * jax.nn.sigmoid(a) * b).astype(BF16)
        acc = acc + _dot(act, w2_ref[f0:f1, :])
    o_ref[...] = acc


def _ffn(h, g, w1, w3, w2):
    n, d = h.shape
    fdim = w1.shape[1]
    tm = min(TOKEN_TILE, n)
    return pl.pallas_call(
        functools.partial(_ffn_kernel, fdim=fdim),
        out_shape=jax.ShapeDtypeStruct((n, d), F32),
        grid=(n // tm,),
        in_specs=[
            pl.BlockSpec((tm, d), lambda i: (i, 0)),
            _const_spec((1, d)),
            _const_spec(w1.shape),
            _const_spec(w3.shape),
            _const_spec(w2.shape),
        ],
        out_specs=pl.BlockSpec((tm, d), lambda i: (i, 0)),
        compiler_params=_params("arbitrary"),
        name="dense_swiglu",
    )(h, g, w1, w3, w2)


def _odd_in_kernel(h_ref, g_ref, wt_ref, o_ref, *, nck):
    hn = _rms(h_ref[...], g_ref[...]).astype(BF16)
    zt = lax.dot_general(wt_ref[...], hn, (((1,), (1,)), ((), ())), preferred_element_type=F32)
    for c in range(nck):
        o_ref[c] = zt[:, c * LANES:(c + 1) * LANES]


def _odd_in(h, g, w_in_t):
    n, d = h.shape
    cols = w_in_t.shape[0]
    tm = min(TOKEN_TILE, n)
    nck = tm // LANES
    return pl.pallas_call(
        functools.partial(_odd_in_kernel, nck=nck),
        out_shape=jax.ShapeDtypeStruct((n // LANES, cols, LANES), F32),
        grid=(n // tm,),
        in_specs=[
            pl.BlockSpec((tm, d), lambda i: (i, 0)),
            _const_spec((1, d)),
            _const_spec(w_in_t.shape),
        ],
        out_specs=pl.BlockSpec((nck, cols, LANES), lambda i: (i, 0, 0)),
        compiler_params=_params("arbitrary"),
        name="odd_in_proj",
    )(h, g, w_in_t)


def _toeplitz_kernel(kt_ref, m_ref, *, hdim):
    t = T_SSM
    row = lax.broadcasted_iota(jnp.int32, (t, t), 0)
    col = lax.broadcasted_iota(jnp.int32, (t, t), 1)
    causal = col >= row

    def body(jj, carry):
        for i in range(hdim):
            kb = jnp.broadcast_to(kt_ref[0, jj, i:i + 1, :], (t, t))
            tz = pltpu.roll(kb, 0, 1, stride=1, stride_axis=0)
            tz = jnp.where(causal, tz, 0.0)
            m_ref[0, pl.ds(pl.multiple_of(jj * t, t), t), i * t:(i + 1) * t] = tz.astype(BF16)
        return carry

    lax.fori_loop(0, hdim, body, 0)


def _toeplitz(kt):
    g, hdim, _, t = kt.shape
    return pl.pallas_call(
        functools.partial(_toeplitz_kernel, hdim=hdim),
        out_shape=jax.ShapeDtypeStruct((g, hdim * t, hdim * t), BF16),
        grid=(g,),
        in_specs=[pl.BlockSpec((1, hdim, hdim, t), lambda i: (i, 0, 0, 0))],
        out_specs=pl.BlockSpec((1, hdim * t, hdim * t), lambda i: (i, 0, 0)),
        compiler_params=_params("arbitrary"),
        name="s5_toeplitz",
    )(kt)


def _s5_kernel(u_ref, m_ref, ws_ref, wc_ref, coef_ref, y_ref, *, hdim, cpb, n_steps):
    t = T_SSM
    nc = u_ref.shape[0]
    x = jnp.concatenate([u_ref[:, c, :] for c in range(hdim)], axis=1).astype(BF16)
    y = _dot(x, m_ref[0])
    st = _dot(x, ws_ref[0])
    half = st.shape[1] // 2
    rb = lax.broadcasted_iota(jnp.int32, st.shape, 0) & (cpb - 1)
    for k in range(n_steps):
        d = 1 << k
        sh = jnp.where(rb >= d, pltpu.roll(st, d, 0), 0.0)
        sw = pltpu.roll(sh, half, 1)
        st = st + coef_ref[0, k, 0:1, :] * sh + coef_ref[0, k, 1:2, :] * sw
    prev = jnp.where(rb >= 1, pltpu.roll(st, 1, 0), 0.0)
    y = y + _dot(prev.astype(BF16), wc_ref[0])
    for i in range(hdim):
        y_ref[:, i, :] = y[:, i * t:(i + 1) * t]


def _s5(zt3, m, ws, wc, coef, *, cpb):
    nc = zt3.shape[0]
    g, ht, _ = m.shape
    hdim = ht // T_SSM
    n_steps = coef.shape[1]
    return pl.pallas_call(
        functools.partial(_s5_kernel, hdim=hdim, cpb=cpb, n_steps=n_steps),
        out_shape=jax.ShapeDtypeStruct((nc, g * hdim, T_SSM), F32),
        grid=(g,),
        in_specs=[
            pl.BlockSpec((nc, hdim, T_SSM), lambda i: (0, i, 0)),
            pl.BlockSpec((1,) + m.shape[1:], lambda i: (i, 0, 0)),
            pl.BlockSpec((1,) + ws.shape[1:], lambda i: (i, 0, 0)),
            pl.BlockSpec((1,) + wc.shape[1:], lambda i: (i, 0, 0)),
            pl.BlockSpec((1,) + coef.shape[1:], lambda i: (i, 0, 0, 0)),
        ],
        out_specs=pl.BlockSpec((nc, hdim, T_SSM), lambda i: (0, i, 0)),
        compiler_params=_params("arbitrary"),
        name="s5_chunked",
    )(zt3, m, ws, wc, coef)


def _s5_tables(lam_re, lam_im, log_dt, b_re, b_im, c_re, c_im, *, cpb):
    hp = lax.Precision.HIGHEST
    g, p = lam_re.shape
    hdim = b_re.shape[2]
    t = T_SSM
    dt = jnp.exp(log_dt)[:, None]
    lr, li = lam_re, lam_im
    mag = jnp.exp(lr * dt)
    abar_r = mag * jnp.cos(li * dt)
    abar_i = mag * jnp.sin(li * dt)
    qr, qi = abar_r - 1.0, abar_i
    den = lr * lr + li * li
    fr = ((qr * lr + qi * li) / den)[..., None]
    fi = ((qi * lr - qr * li) / den)[..., None]
    bbar_r = fr * b_re - fi * b_im
    bbar_i = fr * b_im + fi * b_re

    def apow(k):
        kk = k[None, :, None]
        m = jnp.exp(kk * (lr * dt)[:, None, :])
        th = kk * (li * dt)[:, None, :]
        return m * jnp.cos(th), m * jnp.sin(th)

    lags = jnp.arange(t, dtype=F32)
    pr, pi = apow(lags)
    car = c_re[:, None] * pr[:, :, None, :] - c_im[:, None] * pi[:, :, None, :]
    cai = c_re[:, None] * pi[:, :, None, :] + c_im[:, None] * pr[:, :, None, :]
    kmat = (jnp.einsum('gtip,gpj->gtij', car, bbar_r, precision=hp)
            - jnp.einsum('gtip,gpj->gtij', cai, bbar_i, precision=hp))
    kt = kmat.transpose(0, 3, 2, 1)
    er, ei = pr[:, ::-1], pi[:, ::-1]
    bjr = bbar_r.transpose(0, 2, 1)[:, :, None, :]
    bji = bbar_i.transpose(0, 2, 1)[:, :, None, :]
    ws_r = er[:, None] * bjr - ei[:, None] * bji
    ws_i = er[:, None] * bji + ei[:, None] * bjr
    ws = jnp.concatenate([ws_r, ws_i], axis=-1).reshape(g, hdim * t, 2 * p).astype(BF16)
    p1r, p1i = apow(lags + 1.0)
    ccr = c_re[:, :, None, :] * p1r[:, None] - c_im[:, :, None, :] * p1i[:, None]
    cci = c_re[:, :, None, :] * p1i[:, None] + c_im[:, :, None, :] * p1r[:, None]
    wc = jnp.concatenate([ccr, -cci], axis=-1).reshape(g, hdim * t, 2 * p)
    wc = wc.transpose(0, 2, 1).astype(BF16)
    n_steps = max(1, int(math.log2(cpb)))
    sr, si = apow(float(t) * (2.0 ** jnp.arange(n_steps, dtype=F32)))
    coef = jnp.stack([jnp.concatenate([sr, sr], axis=-1),
                      jnp.concatenate([-si, si], axis=-1)], axis=2)
    return kt, ws, wc, coef


def _odd_tail_kernel(z_ref, y_ref, h_ref, d_ref, gw_ref, gb_ref, lng_ref, sw_ref, sb_ref,
                     wout_ref, g2_ref, rw_ref, o_ref, xn_ref, route_ref,
                     *, nck, sdim, heads, n_experts):
    def cat(ref, r0, r1):
        return jnp.concatenate([ref[c, r0:r1, :] for c in range(nck)], axis=1)

    def rep(ref, r0=None, r1=None):
        v = ref[...] if r0 is None else ref[r0:r1, :]
        return jnp.concatenate([v] * nck, axis=1)

    ut = cat(z_ref, 0, sdim)
    yt = cat(y_ref, 0, sdim)
    gt = jax.nn.gelu(yt + rep(d_ref) * ut)
    glu = _dot(gw_ref[...], gt.astype(BF16)) + rep(gb_ref)
    parts = [gt * jax.nn.sigmoid(glu)]

    hd = sdim // heads
    for hh in range(heads):
        r0 = sdim + hh * hd
        up = jax.nn.gelu(cat(z_ref, r0, r0 + hd))
        vp = jax.nn.gelu(cat(z_ref, sdim + r0, sdim + r0 + hd))
        mu = jnp.mean(vp, axis=0, keepdims=True)
        vc = vp - mu
        var = jnp.mean(vc * vc, axis=0, keepdims=True)
        vn = vc * lax.rsqrt(var + LN_EPS) * rep(lng_ref, hh * hd, (hh + 1) * hd)
        stacked = jnp.concatenate([vn[:, c * LANES:(c + 1) * LANES] for c in range(nck)], axis=0)
        s = _dot(stacked.astype(BF16), sw_ref[hh]) + sb_ref[hh]
        s = jnp.concatenate([s[c * hd:(c + 1) * hd, :] for c in range(nck)], axis=1)
        parts.append(up * s)

    mix = jnp.concatenate(parts, axis=0).T.astype(BF16)
    h1 = h_ref[...] + _dot(mix, wout_ref[...])
    o_ref[...] = h1

    xn = _rms(h1, g2_ref[...])
    xn_ref[...] = xn.astype(BF16)
    logits = jnp.dot(xn, rw_ref[...], precision=lax.Precision.HIGHEST, preferred_element_type=F32)
    lane = lax.broadcasted_iota(jnp.int32, logits.shape, 1)
    neg = jnp.float32(-jnp.inf)
    lg = jnp.where(lane < n_experts, logits, neg)
    m1 = jnp.max(lg, axis=1, keepdims=True)
    i1 = jnp.min(jnp.where(lg == m1, lane, LANES), axis=1, keepdims=True)
    lg2 = jnp.where(lane == i1, neg, lg)
    m2 = jnp.max(lg2, axis=1, keepdims=True)
    i2 = jnp.min(jnp.where(lg2 == m2, lane, LANES), axis=1, keepdims=True)
    e2 = jnp.exp(m2 - m1)
    w1 = 1.0 / (1.0 + e2)
    w2 = e2 / (1.0 + e2)
    route = (jnp.where(lane == i1, w1, 0.0) + jnp.where(lane == i2, w2, 0.0)
             + jnp.where(lane == n_experts, i1.astype(F32), 0.0)
             + jnp.where(lane == n_experts + 1, i2.astype(F32), 0.0))
    route_ref[...] = route


def _odd_tail(zt3, yt3, h, d_b, glu_wt, glu_b_b, lng_b, sgu_wt, sgu_b, w_out, g2, router_pad,
              *, heads, n_experts):
    n, dm = h.shape
    cols = zt3.shape[1]
    sdim = yt3.shape[1]
    tm = min(TOKEN_TILE, n)
    nck = tm // LANES
    kern = functools.partial(_odd_tail_kernel, nck=nck, sdim=sdim, heads=heads, n_experts=n_experts)
    return pl.pallas_call(
        kern,
        out_shape=(jax.ShapeDtypeStruct((n, dm), F32),
                   jax.ShapeDtypeStruct((n, dm), BF16),
                   jax.ShapeDtypeStruct((n, LANES), F32)),
        grid=(n // tm,),
        in_specs=[
            pl.BlockSpec((nck, cols, LANES), lambda i: (i, 0, 0)),
            pl.BlockSpec((nck, sdim, LANES), lambda i: (i, 0, 0)),
            pl.BlockSpec((tm, dm), lambda i: (i, 0)),
            _const_spec(d_b.shape),
            _const_spec(glu_wt.shape),
            _const_spec(glu_b_b.shape),
            _const_spec(lng_b.shape),
            _const_spec(sgu_wt.shape),
            _const_spec(sgu_b.shape),
            _const_spec(w_out.shape),
            _const_spec((1, dm)),
            _const_spec(router_pad.shape),
        ],
        out_specs=(pl.BlockSpec((tm, dm), lambda i: (i, 0)),
                   pl.BlockSpec((tm, dm), lambda i: (i, 0)),
                   pl.BlockSpec((tm, LANES), lambda i: (i, 0))),
        compiler_params=_params("arbitrary"),
        name="odd_mixer_tail",
    )(zt3, yt3, h, d_b, glu_wt, glu_b_b, lng_b, sgu_wt, sgu_b, w_out, g2, router_pad)


def _experts_kernel(te_ref, na_ref, x_ref, gate_ref, w1_ref, w3_ref, w2_ref, o_ref, acc_ref):
    i = pl.program_id(0)
    f = pl.program_id(1)
    nf = pl.num_programs(1)
    active = i < na_ref[0]

    @pl.when(active)
    def _():
        x = x_ref[...]
        a = _dot(x, w1_ref[0])
        b = _dot(x, w3_ref[0])
        act = (a * jax.nn.sigmoid(a) * b).astype(BF16)
        part = _dot(act, w2_ref[0])

        @pl.when(f == 0)
        def _():
            acc_ref[...] = part

        @pl.when(f > 0)
        def _():
            acc_ref[...] = acc_ref[...] + part

    @pl.when(f == nf - 1)
    def _():
        o_ref[...] = jnp.where(active, acc_ref[...] * gate_ref[...], 0.0)


def _experts(tile_expert, n_active, xs, gate_s, w1, w3, w2):
    r, d = xs.shape
    fdim = w1.shape[2]
    te = EXPERT_TILE
    tf = min(EXPERT_F_TILE, fdim)
    n_tiles = r // te
    nf = fdim // tf

    def row_map(i, f, te_ref, na_ref):
        return (jnp.minimum(i, na_ref[0] - 1), 0)

    grid_spec = pltpu.PrefetchScalarGridSpec(
        num_scalar_prefetch=2,
        grid=(n_tiles, nf),
        in_specs=[
            pl.BlockSpec((te, d), row_map),
            pl.BlockSpec((te, 1), row_map),
            pl.BlockSpec((1, d, tf), lambda i, f, te_ref, na_ref: (te_ref[i], 0, f)),
            pl.BlockSpec((1, d, tf), lambda i, f, te_ref, na_ref: (te_ref[i], 0, f)),
            pl.BlockSpec((1, tf, d), lambda i, f, te_ref, na_ref: (te_ref[i], f, 0)),
        ],
        out_specs=pl.BlockSpec((te, d), lambda i, f, te_ref, na_ref: (i, 0)),
        scratch_shapes=[pltpu.VMEM((te, d), F32)],
    )
    return pl.pallas_call(
        _experts_kernel,
        out_shape=jax.ShapeDtypeStruct((r, d), F32),
        grid_spec=grid_spec,
        compiler_params=_params("arbitrary", "arbitrary"),
        name="grouped_experts",
    )(tile_expert, n_active, xs, gate_s, w1, w3, w2)


def _moe(h1, xn, route, w1, w3, w2):
    n, d = h1.shape
    n_exp = w1.shape[0]
    te = EXPERT_TILE
    n_tiles = (n * TOP_K) // te + n_exp
    r = n_tiles * te

    gates = route[:, :n_exp]
    ids = route[:, n_exp:n_exp + TOP_K].astype(jnp.int32)
    sel = (ids[:, :, None] == jnp.arange(n_exp, dtype=jnp.int32)[None, None, :]).any(axis=1)
    csum = jnp.cumsum(sel.astype(jnp.int32), axis=0)
    counts = csum[-1]
    padded = ((counts + te - 1) // te) * te
    ends = jnp.cumsum(padded)
    offs = ends - padded
    dest = offs[None, :] + csum - 1
    tok = jnp.broadcast_to(jnp.arange(n, dtype=jnp.int32)[:, None], (n, n_exp))
    src = jnp.zeros((r,), jnp.int32).at[jnp.where(sel, dest, r).reshape(-1)].set(
        tok.reshape(-1), mode='drop')
    tile_start = jnp.arange(n_tiles, dtype=jnp.int32) * te
    n_active = (ends[-1] // te).astype(jnp.int32)
    tile_expert = jnp.sum(tile_start[:, None] >= ends[None, :], axis=1).astype(jnp.int32)
    last_expert = jnp.take(tile_expert, jnp.maximum(n_active - 1, 0))
    tile_expert = jnp.where(tile_start < ends[-1], tile_expert, last_expert)
    row_expert = jnp.repeat(tile_expert, te)
    gate_s = gates[src, row_expert][:, None]

    xs = jnp.take(xn, src, axis=0)
    ys = _experts(tile_expert, n_active.reshape(1), xs, gate_s, w1, w3, w2)
    pos = jnp.take_along_axis(dest, ids, axis=1)
    return h1 + jnp.take(ys, pos[:, 0], axis=0) + jnp.take(ys, pos[:, 1], axis=0)


def _final_norm_kernel(h_ref, g_ref, o_ref):
    o_ref[...] = _rms(h_ref[...], g_ref[...])


def _final_norm(h, g):
    n, d = h.shape
    tm = min(TOKEN_TILE, n)
    return pl.pallas_call(
        _final_norm_kernel,
        out_shape=jax.ShapeDtypeStruct((n, d), F32),
        grid=(n // tm,),
        in_specs=[pl.BlockSpec((tm, d), lambda i: (i, 0)), _const_spec((1, d))],
        out_specs=pl.BlockSpec((tm, d), lambda i: (i, 0)),
        compiler_params=_params("arbitrary"),
        name="final_norm",
    )(h, g)


def _even_layer(h, norm1_g, w_in, conv_w, pool_w, pool_scale, w_out, norm2_g, w1, w3, w2,
                *, bsz, seq):
    h = _even_mixer(h, norm1_g[None], w_in.astype(BF16), conv_w, pool_w.astype(BF16),
                    pool_scale[None], w_out.astype(BF16), bsz=bsz, seq=seq)
    return _ffn(h, norm2_g[None], w1.astype(BF16), w3.astype(BF16), w2.astype(BF16))


def _odd_layer(h, norm1_g, w_in, lam_re, lam_im, log_dt, b_re, b_im, c_re, c_im, d_skip,
               glu_w, glu_b, ln_g, sgu_w, sgu_b, w_out, norm2_g, router_w, w1, w3, w2, *, seq):
    sdim = d_skip.shape[0]
    heads, slen, _ = sgu_w.shape
    n_exp = router_w.shape[1]
    assert slen == LANES and seq % T_SSM == 0
    cpb = seq // T_SSM
    assert cpb & (cpb - 1) == 0, "chunks per sequence must be a power of two"

    zt3 = _odd_in(h, norm1_g[None], w_in.T.astype(BF16))
    kt, ws, wc, coef = _s5_tables(lam_re, lam_im, log_dt, b_re, b_im, c_re, c_im, cpb=cpb)
    yt3 = _s5(zt3, _toeplitz(kt), ws, wc, coef, cpb=cpb)

    lanes_b = lambda v: jnp.broadcast_to(v[:, None], (v.shape[0], LANES))
    cidx = jnp.arange(slen) // CHUNK
    mask = cidx[None, :] <= cidx[:, None]
    sgu_wt = jnp.where(mask[None], sgu_w, 0.0).transpose(0, 2, 1).astype(BF16)
    router_pad = jnp.zeros((router_w.shape[0], LANES), F32).at[:, :n_exp].set(router_w)
    h1, xn, route = _odd_tail(
        zt3, yt3, h, lanes_b(d_skip), glu_w.T.astype(BF16), lanes_b(glu_b), lanes_b(ln_g),
        sgu_wt, sgu_b[:, None, :], w_out.astype(BF16), norm2_g[None], router_pad,
        heads=heads, n_experts=n_exp)
    return _moe(h1, xn, route, w1.astype(BF16), w3.astype(BF16), w2.astype(BF16))


def kernel(x, ev_norm1_g, ev_w_in, ev_conv_w, ev_pool_w, ev_pool_scale, ev_w_out, ev_norm2_g, ev_ffn_w1, ev_ffn_w3, ev_ffn_w2, od_norm1_g, od_w_in, od_lambda_re, od_lambda_im, od_log_dt, od_b_re, od_b_im, od_c_re, od_c_im, od_d, od_glu_w, od_glu_b, od_sgu_ln_g, od_sgu_w, od_sgu_b, od_w_out, od_norm2_g, od_router_w, od_moe_w1, od_moe_w3, od_moe_w2, final_norm_g):
    bsz, seq, d = x.shape
    depth = ev_norm1_g.shape[0] + od_norm1_g.shape[0]
    h = x.reshape(bsz * seq, d)
    for layer in range(depth):
        i = layer // 2
        if layer % 2 == 0:
            h = _even_layer(h, ev_norm1_g[i], ev_w_in[i], ev_conv_w[i], ev_pool_w[i],
                            ev_pool_scale[i], ev_w_out[i], ev_norm2_g[i], ev_ffn_w1[i],
                            ev_ffn_w3[i], ev_ffn_w2[i], bsz=bsz, seq=seq)
        else:
            h = _odd_layer(h, od_norm1_g[i], od_w_in[i], od_lambda_re[i], od_lambda_im[i],
                           od_log_dt[i], od_b_re[i], od_b_im[i], od_c_re[i], od_c_im[i], od_d[i],
                           od_glu_w[i], od_glu_b[i], od_sgu_ln_g[i], od_sgu_w[i], od_sgu_b[i],
                           od_w_out[i], od_norm2_g[i], od_router_w[i], od_moe_w1[i],
                           od_moe_w3[i], od_moe_w2[i], seq=seq)
    return _final_norm(h, final_norm_g[None]).reshape(bsz, seq, d)
```

```python
import functools
import math

import jax
import jax.numpy as jnp
from jax import lax
from jax.experimental import pallas as pl
from jax.experimental.pallas import tpu as pltpu

F32 = jnp.float32
BF16 = jnp.bfloat16

RMS_EPS = 1e-5
LN_EPS = 1e-5
CHUNK = 64
POOL_WINDOWS = (2, 4, 8, 16)
CONV_WIDTH = 3
TOP_K = 2

LANES = 128
V7X_VMEM_BYTES = 64 * 1024 * 1024
VMEM_LIMIT = V7X_VMEM_BYTES * 7 // 8
HALO = 16
T_SSM = LANES
TOKEN_TILE = 512
EXPERT_TILE = 512
EXPERT_F_TILE = 1792
FFN_F_TILE = 1024


def _const_spec(shape):
    nd = len(shape)
    return pl.BlockSpec(shape, lambda *_: (0,) * nd, pipeline_mode=pl.Buffered(1))


def _layer_spec(arr, layer):
    nd = arr.ndim - 1
    return pl.BlockSpec((None,) + arr.shape[1:], lambda *_: (layer,) + (0,) * nd,
                        pipeline_mode=pl.Buffered(1))


def _params(*sem):
    return pltpu.CompilerParams(dimension_semantics=sem, vmem_limit_bytes=VMEM_LIMIT)


def _rms(x, g):
    return x * lax.rsqrt(jnp.mean(x * x, axis=-1, keepdims=True) + RMS_EPS) * g


def _dot(a, b):
    return jnp.dot(a, b, preferred_element_type=F32)


def _even_mixer_kernel(h_ref, g_ref, win_ref, cw_ref, pw_ref, ps_ref, wout_ref, o_ref,
                       zc_ext, p_ext, *, tm, cdim):
    j = pl.program_id(1)

    @pl.when(j == 0)
    def _():
        zc_ext[0:HALO, :] = jnp.zeros((HALO, cdim), F32)
        p_ext[0:HALO, :] = jnp.zeros((HALO, cdim), F32)

    h = h_ref[...]
    hn = _rms(h, g_ref[...]).astype(BF16)
    z = _dot(hn, win_ref[...])
    gb = z[:, 0:cdim]
    zc = z[:, cdim:2 * cdim] * z[:, 2 * cdim:3 * cdim]
    p = z[:, 3 * cdim:4 * cdim]
    zc_ext[HALO:HALO + tm, :] = zc
    p_ext[HALO:HALO + tm, :] = p

    y = cw_ref[CONV_WIDTH - 1:CONV_WIDTH, :] * zc
    for k in range(CONV_WIDTH - 1):
        sh = CONV_WIDTH - 1 - k
        y = y + cw_ref[k:k + 1, :] * zc_ext[HALO - sh:HALO - sh + tm, :]
    parts = [(gb * y).astype(BF16)]

    pos = (j * tm + 1 + lax.broadcasted_iota(jnp.int32, (tm, LANES), 0)).astype(F32)
    for gi, w in enumerate(POOL_WINDOWS):
        c0 = gi * LANES
        pg = p[:, c0:c0 + LANES]
        s = pg
        for k in range(1, w):
            s = s + p_ext[HALO - k:HALO - k + tm, c0:c0 + LANES]
        pooled = s / jnp.minimum(pos, float(w)) - pg
        mixed = _dot(pooled.astype(BF16), pw_ref[gi]) * ps_ref[:, c0:c0 + LANES]
        parts.append(mixed.astype(BF16))

    mix = jnp.concatenate(parts, axis=1)
    o_ref[...] = h + _dot(mix, wout_ref[...])

    zc_ext[0:HALO, :] = zc_ext[tm:tm + HALO, :]
    p_ext[0:HALO, :] = p_ext[tm:tm + HALO, :]


def _even_mixer(h, g, w_in, conv_w, pool_w, pool_scale, w_out, *, layer, bsz, seq):
    n, d = h.shape
    cdim = w_in.shape[2] // 4
    tm = min(TOKEN_TILE, seq)
    nj = seq // tm
    kern = functools.partial(_even_mixer_kernel, tm=tm, cdim=cdim)
    return pl.pallas_call(
        kern,
        out_shape=jax.ShapeDtypeStruct((n, d), F32),
        grid=(bsz, nj),
        in_specs=[
            pl.BlockSpec((tm, d), lambda b, j: (b * nj + j, 0)),
            _layer_spec(g, layer),
            _layer_spec(w_in, layer),
            _layer_spec(conv_w, layer),
            _layer_spec(pool_w, layer),
            _layer_spec(pool_scale, layer),
            _layer_spec(w_out, layer),
        ],
        out_specs=pl.BlockSpec((tm, d), lambda b, j: (b * nj + j, 0)),
        scratch_shapes=[pltpu.VMEM((HALO + tm, cdim), F32), pltpu.VMEM((HALO + tm, cdim), F32)],
        compiler_params=_params("arbitrary", "arbitrary"),
        name="even_mixer",
    )(h, g, w_in, conv_w, pool_w, pool_scale, w_out)


def _ffn_kernel(h_ref, g_ref, w1_ref, w3_ref, w2_ref, o_ref, *, fdim):
    h = h_ref[...]
    hn = _rms(h, g_ref[...]).astype(BF16)
    acc = h
    for f0 in range(0, fdim, FFN_F_TILE):
        f1 = min(f0 + FFN_F_TILE, fdim)
        a = _dot(hn, w1_ref[:, f0:f1])
        b = _dot(hn, w3_ref[:, f0:f1])
        act = (a * jax.nn.sigmoid(a) * b).astype(BF16)
        acc = acc + _dot(act, w2_ref[f0:f1, :])
    o_ref[...] = acc


def _ffn(h, g, w1, w3, w2, *, layer):
    n, d = h.shape
    fdim = w1.shape[2]
    tm = min(TOKEN_TILE, n)
    return pl.pallas_call(
        functools.partial(_ffn_kernel, fdim=fdim),
        out_shape=jax.ShapeDtypeStruct((n, d), F32),
        grid=(n // tm,),
        in_specs=[
            pl.BlockSpec((tm, d), lambda i: (i, 0)),
            _layer_spec(g, layer),
            _layer_spec(w1, layer),
            _layer_spec(w3, layer),
            _layer_spec(w2, layer),
        ],
        out_specs=pl.BlockSpec((tm, d), lambda i: (i, 0)),
        compiler_params=_params("arbitrary"),
        name="dense_swiglu",
    )(h, g, w1, w3, w2)


def _odd_in_kernel(h_ref, g_ref, wt_ref, o_ref, *, nck):
    hn = _rms(h_ref[...], g_ref[...]).astype(BF16)
    zt = lax.dot_general(wt_ref[...], hn, (((1,), (1,)), ((), ())), preferred_element_type=F32)
    for c in range(nck):
        o_ref[c] = zt[:, c * LANES:(c + 1) * LANES]


def _odd_in(h, g, w_in_t, *, layer):
    n, d = h.shape
    cols = w_in_t.shape[1]
    tm = min(TOKEN_TILE, n)
    nck = tm // LANES
    return pl.pallas_call(
        functools.partial(_odd_in_kernel, nck=nck),
        out_shape=jax.ShapeDtypeStruct((n // LANES, cols, LANES), F32),
        grid=(n // tm,),
        in_specs=[
            pl.BlockSpec((tm, d), lambda i: (i, 0)),
            _layer_spec(g, layer),
            _layer_spec(w_in_t, layer),
        ],
        out_specs=pl.BlockSpec((nck, cols, LANES), lambda i: (i, 0, 0)),
        compiler_params=_params("arbitrary"),
        name="odd_in_proj",
    )(h, g, w_in_t)


def _s5_kernel(u_ref, kt_ref, ws_ref, wc_ref, coef_ref, y_ref, m_scr, *, hdim, cpb, n_steps):
    t = T_SSM
    row = lax.broadcasted_iota(jnp.int32, (t, t), 0)
    col = lax.broadcasted_iota(jnp.int32, (t, t), 1)
    causal = col >= row
    for j in range(hdim):
        for i in range(hdim):
            kb = jnp.broadcast_to(kt_ref[0, j, i:i + 1, :], (t, t))
            tz = pltpu.roll(kb, 0, 1, stride=1, stride_axis=0)
            m_scr[j * t:(j + 1) * t, i * t:(i + 1) * t] = jnp.where(causal, tz, 0.0).astype(BF16)

    x = jnp.concatenate([u_ref[:, c, :] for c in range(hdim)], axis=1).astype(BF16)
    y = _dot(x, m_scr[...])
    st = _dot(x, ws_ref[0])
    half = st.shape[1] // 2
    rb = lax.broadcasted_iota(jnp.int32, st.shape, 0) & (cpb - 1)
    for k in range(n_steps):
        d = 1 << k
        sh = jnp.where(rb >= d, pltpu.roll(st, d, 0), 0.0)
        sw = pltpu.roll(sh, half, 1)
        st = st + coef_ref[0, k, 0:1, :] * sh + coef_ref[0, k, 1:2, :] * sw
    prev = jnp.where(rb >= 1, pltpu.roll(st, 1, 0), 0.0)
    y = y + _dot(prev.astype(BF16), wc_ref[0])
    for i in range(hdim):
        y_ref[:, i, :] = y[:, i * t:(i + 1) * t]


def _s5(zt3, kt, ws, wc, coef, *, cpb):
    nc = zt3.shape[0]
    g, hdim = kt.shape[0], kt.shape[1]
    n_steps = coef.shape[1]
    return pl.pallas_call(
        functools.partial(_s5_kernel, hdim=hdim, cpb=cpb, n_steps=n_steps),
        out_shape=jax.ShapeDtypeStruct((nc, g * hdim, T_SSM), F32),
        grid=(g,),
        in_specs=[
            pl.BlockSpec((nc, hdim, T_SSM), lambda i: (0, i, 0)),
            pl.BlockSpec((1,) + kt.shape[1:], lambda i: (i, 0, 0, 0)),
            pl.BlockSpec((1,) + ws.shape[1:], lambda i: (i, 0, 0)),
            pl.BlockSpec((1,) + wc.shape[1:], lambda i: (i, 0, 0)),
            pl.BlockSpec((1,) + coef.shape[1:], lambda i: (i, 0, 0, 0)),
        ],
        out_specs=pl.BlockSpec((nc, hdim, T_SSM), lambda i: (0, i, 0)),
        scratch_shapes=[pltpu.VMEM((hdim * T_SSM, hdim * T_SSM), BF16)],
        compiler_params=_params("arbitrary"),
        name="s5_chunked",
    )(zt3, kt, ws, wc, coef)


def _s5_tables(lam_re, lam_im, log_dt, b_re, b_im, c_re, c_im, *, cpb):
    hp = lax.Precision.HIGHEST
    g, p = lam_re.shape
    hdim = b_re.shape[2]
    t = T_SSM
    dt = jnp.exp(log_dt)[:, None]
    lr, li = lam_re, lam_im
    mag = jnp.exp(lr * dt)
    abar_r = mag * jnp.cos(li * dt)
    abar_i = mag * jnp.sin(li * dt)
    qr, qi = abar_r - 1.0, abar_i
    den = lr * lr + li * li
    fr = ((qr * lr + qi * li) / den)[..., None]
    fi = ((qi * lr - qr * li) / den)[..., None]
    bbar_r = fr * b_re - fi * b_im
    bbar_i = fr * b_im + fi * b_re

    def apow(k):
        kk = k[None, :, None]
        m = jnp.exp(kk * (lr * dt)[:, None, :])
        th = kk * (li * dt)[:, None, :]
        return m * jnp.cos(th), m * jnp.sin(th)

    lags = jnp.arange(t, dtype=F32)
    pr, pi = apow(lags)
    car = c_re[:, None] * pr[:, :, None, :] - c_im[:, None] * pi[:, :, None, :]
    cai = c_re[:, None] * pi[:, :, None, :] + c_im[:, None] * pr[:, :, None, :]
    kmat = (jnp.einsum('gtip,gpj->gtij', car, bbar_r, precision=hp)
            - jnp.einsum('gtip,gpj->gtij', cai, bbar_i, precision=hp))
    kt = kmat.transpose(0, 3, 2, 1)
    er, ei = pr[:, ::-1], pi[:, ::-1]
    bjr = bbar_r.transpose(0, 2, 1)[:, :, None, :]
    bji = bbar_i.transpose(0, 2, 1)[:, :, None, :]
    ws_r = er[:, None] * bjr - ei[:, None] * bji
    ws_i = er[:, None] * bji + ei[:, None] * bjr
    ws = jnp.concatenate([ws_r, ws_i], axis=-1).reshape(g, hdim * t, 2 * p).astype(BF16)
    p1r, p1i = apow(lags + 1.0)
    ccr = c_re[:, :, None, :] * p1r[:, None] - c_im[:, :, None, :] * p1i[:, None]
    cci = c_re[:, :, None, :] * p1i[:, None] + c_im[:, :, None, :] * p1r[:, None]
    wc = jnp.concatenate([ccr, -cci], axis=-1).reshape(g, hdim * t, 2 * p)
    wc = wc.transpose(0, 2, 1).astype(BF16)
    n_steps = max(1, int(math.log2(cpb)))
    sr, si = apow(float(t) * (2.0 ** jnp.arange(n_steps, dtype=F32)))
    coef = jnp.stack([jnp.concatenate([sr, sr], axis=-1),
                      jnp.concatenate([-si, si], axis=-1)], axis=2)
    return kt, ws, wc, coef


def _odd_tail_kernel(z_ref, y_ref, h_ref, d_ref, gw_ref, gb_ref, lng_ref, sw_ref, sb_ref,
                     wout_ref, g2_ref, rw_ref, o_ref, xr_ref,
                     *, nck, sdim, heads, n_experts):
    def cat(ref, r0, r1):
        return jnp.concatenate([ref[c, r0:r1, :] for c in range(nck)], axis=1)

    def rep(ref, r0=None, r1=None):
        v = ref[...] if r0 is None else ref[r0:r1, :]
        return jnp.concatenate([v] * nck, axis=1)

    ut = cat(z_ref, 0, sdim)
    yt = cat(y_ref, 0, sdim)
    gt = jax.nn.gelu(yt + rep(d_ref) * ut)
    glu = _dot(gw_ref[...], gt.astype(BF16)) + rep(gb_ref)
    parts = [gt * jax.nn.sigmoid(glu)]

    hd = sdim // heads
    for hh in range(heads):
        r0 = sdim + hh * hd
        up = jax.nn.gelu(cat(z_ref, r0, r0 + hd))
        vp = jax.nn.gelu(cat(z_ref, sdim + r0, sdim + r0 + hd))
        mu = jnp.mean(vp, axis=0, keepdims=True)
        vc = vp - mu
        var = jnp.mean(vc * vc, axis=0, keepdims=True)
        vn = vc * lax.rsqrt(var + LN_EPS) * rep(lng_ref, hh * hd, (hh + 1) * hd)
        stacked = jnp.concatenate([vn[:, c * LANES:(c + 1) * LANES] for c in range(nck)], axis=0)
        s = _dot(stacked.astype(BF16), sw_ref[hh]) + sb_ref[hh]
        s = jnp.concatenate([s[c * hd:(c + 1) * hd, :] for c in range(nck)], axis=1)
        parts.append(up * s)

    mix = jnp.concatenate(parts, axis=0).T.astype(BF16)
    h1 = h_ref[...] + _dot(mix, wout_ref[...])
    o_ref[...] = h1

    dm = h1.shape[1]
    xn = _rms(h1, g2_ref[...])
    xr_ref[:, 0:dm] = xn
    logits = jnp.dot(xn, rw_ref[...], precision=lax.Precision.HIGHEST, preferred_element_type=F32)
    lane = lax.broadcasted_iota(jnp.int32, logits.shape, 1)
    neg = jnp.float32(-jnp.inf)
    lg = jnp.where(lane < n_experts, logits, neg)
    m1 = jnp.max(lg, axis=1, keepdims=True)
    i1 = jnp.min(jnp.where(lg == m1, lane, LANES), axis=1, keepdims=True)
    lg2 = jnp.where(lane == i1, neg, lg)
    m2 = jnp.max(lg2, axis=1, keepdims=True)
    i2 = jnp.min(jnp.where(lg2 == m2, lane, LANES), axis=1, keepdims=True)
    e2 = jnp.exp(m2 - m1)
    w1 = 1.0 / (1.0 + e2)
    w2 = e2 / (1.0 + e2)
    route = (jnp.where(lane == i1, w1, 0.0) + jnp.where(lane == i2, w2, 0.0)
             + jnp.where(lane == n_experts, i1.astype(F32), 0.0)
             + jnp.where(lane == n_experts + 1, i2.astype(F32), 0.0))
    xr_ref[:, dm:dm + LANES] = route


def _odd_tail(zt3, yt3, h, d_b, glu_wt, glu_b_b, lng_b, sgu_wt, sgu_b, w_out, g2, router_pad,
              *, layer, heads, n_experts):
    n, dm = h.shape
    cols = zt3.shape[1]
    sdim = yt3.shape[1]
    tm = min(TOKEN_TILE, n)
    nck = tm // LANES
    kern = functools.partial(_odd_tail_kernel, nck=nck, sdim=sdim, heads=heads, n_experts=n_experts)
    return pl.pallas_call(
        kern,
        out_shape=(jax.ShapeDtypeStruct((n, dm), F32),
                   jax.ShapeDtypeStruct((n, dm + LANES), F32)),
        grid=(n // tm,),
        in_specs=[
            pl.BlockSpec((nck, cols, LANES), lambda i: (i, 0, 0)),
            pl.BlockSpec((nck, sdim, LANES), lambda i: (i, 0, 0)),
            pl.BlockSpec((tm, dm), lambda i: (i, 0)),
            _const_spec(d_b.shape),
            _layer_spec(glu_wt, layer),
            _const_spec(glu_b_b.shape),
            _const_spec(lng_b.shape),
            _const_spec(sgu_wt.shape),
            _const_spec(sgu_b.shape),
            _layer_spec(w_out, layer),
            _layer_spec(g2, layer),
            _const_spec(router_pad.shape),
        ],
        out_specs=(pl.BlockSpec((tm, dm), lambda i: (i, 0)),
                   pl.BlockSpec((tm, dm + LANES), lambda i: (i, 0))),
        compiler_params=_params("arbitrary"),
        name="odd_mixer_tail",
    )(zt3, yt3, h, d_b, glu_wt, glu_b_b, lng_b, sgu_wt, sgu_b, w_out, g2, router_pad)


def _experts_kernel(te_ref, na_ref, xr_ref, w1_ref, w3_ref, w2_ref, o_ref, acc_ref, *, dm):
    i = pl.program_id(0)
    f = pl.program_id(1)
    nf = pl.num_programs(1)
    active = i < na_ref[0]

    @pl.when(active)
    def _():
        x = xr_ref[:, 0:dm].astype(BF16)
        a = _dot(x, w1_ref[0])
        b = _dot(x, w3_ref[0])
        act = (a * jax.nn.sigmoid(a) * b).astype(BF16)
        part = _dot(act, w2_ref[0])

        @pl.when(f == 0)
        def _():
            acc_ref[...] = part

        @pl.when(f > 0)
        def _():
            acc_ref[...] = acc_ref[...] + part

    @pl.when(f == nf - 1)
    def _():
        routing = xr_ref[:, dm:dm + LANES]
        lane = lax.broadcasted_iota(jnp.int32, routing.shape, 1)
        gate = jnp.sum(jnp.where(lane == te_ref[i], routing, 0.0), axis=1, keepdims=True)
        o_ref[...] = jnp.where(active, acc_ref[...] * gate, 0.0)


def _experts(tile_expert, n_active, xrs, w1, w3, w2, *, layer):
    r = xrs.shape[0]
    d, fdim = w1.shape[2], w1.shape[3]
    te = EXPERT_TILE
    tf = min(EXPERT_F_TILE, fdim)
    n_tiles = r // te
    nf = fdim // tf

    grid_spec = pltpu.PrefetchScalarGridSpec(
        num_scalar_prefetch=2,
        grid=(n_tiles, nf),
        in_specs=[
            pl.BlockSpec((te, d + LANES),
                         lambda i, f, te_ref, na_ref: (jnp.minimum(i, na_ref[0] - 1), 0)),
            pl.BlockSpec((None, 1, d, tf), lambda i, f, te_ref, na_ref: (layer, te_ref[i], 0, f)),
            pl.BlockSpec((None, 1, d, tf), lambda i, f, te_ref, na_ref: (layer, te_ref[i], 0, f)),
            pl.BlockSpec((None, 1, tf, d), lambda i, f, te_ref, na_ref: (layer, te_ref[i], f, 0)),
        ],
        out_specs=pl.BlockSpec((te, d), lambda i, f, te_ref, na_ref: (i, 0)),
        scratch_shapes=[pltpu.VMEM((te, d), F32)],
    )
    return pl.pallas_call(
        functools.partial(_experts_kernel, dm=d),
        out_shape=jax.ShapeDtypeStruct((r, d), F32),
        grid_spec=grid_spec,
        compiler_params=_params("arbitrary", "arbitrary"),
        name="grouped_experts",
    )(tile_expert, n_active, xrs, w1, w3, w2)


def _moe(h1, xr, w1, w3, w2, *, layer):
    n, d = h1.shape
    n_exp = w1.shape[1]
    te = EXPERT_TILE
    n_tiles = (n * TOP_K) // te + n_exp
    nb = n // LANES

    ids = xr[:, d + n_exp:d + n_exp + TOP_K].astype(jnp.int32)
    sel = (ids[:, :, None] == jnp.arange(n_exp, dtype=jnp.int32)[None, None, :]).any(axis=1)
    csum = jnp.cumsum(sel.astype(jnp.int32), axis=0)
    counts = csum[-1]
    padded = ((counts + te - 1) // te) * te
    ends = jnp.cumsum(padded)
    offs = ends - padded
    tile_start = jnp.arange(n_tiles, dtype=jnp.int32) * te
    n_active = (ends[-1] // te).astype(jnp.int32)
    tile_expert = jnp.sum(tile_start[:, None] >= ends[None, :], axis=1).astype(jnp.int32)
    last_expert = jnp.take(tile_expert, jnp.maximum(n_active - 1, 0))
    tile_expert = jnp.where(tile_start < ends[-1], tile_expert, last_expert)

    csum_t = csum.T.reshape(n_exp, nb, LANES)
    rank = (tile_start[:, None] + jnp.arange(te, dtype=jnp.int32)[None, :]
            - offs[tile_expert][:, None])
    blk_end = csum_t[:, :, LANES - 1][tile_expert]
    blk = jnp.sum(blk_end[:, None, :] <= rank[:, :, None], axis=2)
    blk = jnp.minimum(blk, nb - 1)
    rows = jnp.take(csum_t.reshape(n_exp * nb, LANES),
                    (tile_expert[:, None] * nb + blk).reshape(-1), axis=0, mode='clip')
    within = jnp.sum(rows <= rank.reshape(-1, 1), axis=1)
    src = jnp.minimum(blk.reshape(-1) * LANES + within, n - 1).astype(jnp.int32)

    xrs = jnp.take(xr, src, axis=0, mode='clip')
    ys = _experts(tile_expert, n_active.reshape(1), xrs, w1, w3, w2, layer=layer)
    dest = offs[None, :] + csum - 1
    pos = jnp.take_along_axis(dest, ids, axis=1)
    return (h1 + jnp.take(ys, pos[:, 0], axis=0, mode='clip')
            + jnp.take(ys, pos[:, 1], axis=0, mode='clip'))


def _final_norm_kernel(h_ref, g_ref, o_ref):
    o_ref[...] = _rms(h_ref[...], g_ref[...])


def _final_norm(h, g):
    n, d = h.shape
    tm = min(TOKEN_TILE, n)
    return pl.pallas_call(
        _final_norm_kernel,
        out_shape=jax.ShapeDtypeStruct((n, d), F32),
        grid=(n // tm,),
        in_specs=[pl.BlockSpec((tm, d), lambda i: (i, 0)), _const_spec((1, d))],
        out_specs=pl.BlockSpec((tm, d), lambda i: (i, 0)),
        compiler_params=_params("arbitrary"),
        name="final_norm",
    )(h, g)


def _even_layer(h, p, i, *, bsz, seq):
    h = _even_mixer(h, p['ev_norm1_g'], p['ev_w_in'], p['ev_conv_w'], p['ev_pool_w'],
                    p['ev_pool_scale'], p['ev_w_out'], layer=i, bsz=bsz, seq=seq)
    return _ffn(h, p['ev_norm2_g'], p['ev_ffn_w1'], p['ev_ffn_w3'], p['ev_ffn_w2'], layer=i)


def _odd_layer(h, p, i, *, seq):
    sgu_w = p['od_sgu_w'][i]
    router_w = p['od_router_w'][i]
    heads, slen, _ = sgu_w.shape
    n_exp = router_w.shape[1]
    assert slen == LANES and seq % T_SSM == 0
    cpb = seq // T_SSM
    assert cpb & (cpb - 1) == 0, "chunks per sequence must be a power of two"

    zt3 = _odd_in(h, p['od_norm1_g'], p['od_w_in_t'], layer=i)
    kt, ws, wc, coef = _s5_tables(p['od_lambda_re'][i], p['od_lambda_im'][i], p['od_log_dt'][i],
                                  p['od_b_re'][i], p['od_b_im'][i], p['od_c_re'][i],
                                  p['od_c_im'][i], cpb=cpb)
    yt3 = _s5(zt3, kt, ws, wc, coef, cpb=cpb)

    lanes_b = lambda v: jnp.broadcast_to(v[:, None], (v.shape[0], LANES))
    cidx = jnp.arange(slen) // CHUNK
    mask = cidx[None, :] <= cidx[:, None]
    sgu_wt = jnp.where(mask[None], sgu_w, 0.0).transpose(0, 2, 1).astype(BF16)
    router_pad = jnp.zeros((router_w.shape[0], LANES), F32).at[:, :n_exp].set(router_w)
    h1, xr = _odd_tail(
        zt3, yt3, h, lanes_b(p['od_d'][i]), p['od_glu_wt'], lanes_b(p['od_glu_b'][i]),
        lanes_b(p['od_sgu_ln_g'][i]), sgu_wt, p['od_sgu_b'][i][:, None, :], p['od_w_out'],
        p['od_norm2_g'], router_pad, layer=i, heads=heads, n_experts=n_exp)
    return _moe(h1, xr, p['od_moe_w1'], p['od_moe_w3'], p['od_moe_w2'], layer=i)


def kernel(x, ev_norm1_g, ev_w_in, ev_conv_w, ev_pool_w, ev_pool_scale, ev_w_out, ev_norm2_g, ev_ffn_w1, ev_ffn_w3, ev_ffn_w2, od_norm1_g, od_w_in, od_lambda_re, od_lambda_im, od_log_dt, od_b_re, od_b_im, od_c_re, od_c_im, od_d, od_glu_w, od_glu_b, od_sgu_ln_g, od_sgu_w, od_sgu_b, od_w_out, od_norm2_g, od_router_w, od_moe_w1, od_moe_w3, od_moe_w2, final_norm_g):
    bsz, seq, d = x.shape
    depth = ev_norm1_g.shape[0] + od_norm1_g.shape[0]
    row = lambda v: v[:, None, :]
    p = dict(
        ev_norm1_g=row(ev_norm1_g), ev_w_in=ev_w_in.astype(BF16), ev_conv_w=ev_conv_w,
        ev_pool_w=ev_pool_w.astype(BF16), ev_pool_scale=row(ev_pool_scale),
        ev_w_out=ev_w_out.astype(BF16), ev_norm2_g=row(ev_norm2_g),
        ev_ffn_w1=ev_ffn_w1.astype(BF16), ev_ffn_w3=ev_ffn_w3.astype(BF16),
        ev_ffn_w2=ev_ffn_w2.astype(BF16),
        od_norm1_g=row(od_norm1_g), od_w_in_t=od_w_in.transpose(0, 2, 1).astype(BF16),
        od_lambda_re=od_lambda_re, od_lambda_im=od_lambda_im, od_log_dt=od_log_dt,
        od_b_re=od_b_re, od_b_im=od_b_im, od_c_re=od_c_re, od_c_im=od_c_im, od_d=od_d,
        od_glu_wt=od_glu_w.transpose(0, 2, 1).astype(BF16), od_glu_b=od_glu_b,
        od_sgu_ln_g=od_sgu_ln_g, od_sgu_w=od_sgu_w, od_sgu_b=od_sgu_b,
        od_w_out=od_w_out.astype(BF16), od_norm2_g=row(od_norm2_g), od_router_w=od_router_w,
        od_moe_w1=od_moe_w1.astype(BF16), od_moe_w3=od_moe_w3.astype(BF16),
        od_moe_w2=od_moe_w2.astype(BF16),
    )
    h = x.reshape(bsz * seq, d)
    for layer in range(depth):
        if layer % 2 == 0:
            h = _even_layer(h, p, layer // 2, bsz=bsz, seq=seq)
        else:
            h = _odd_layer(h, p, layer // 2, seq=seq)
    return _final_norm(h, final_norm_g[None]).reshape(bsz, seq, d)
```

```python
import functools
import math

import jax
import jax.numpy as jnp
from jax import lax
from jax.experimental import pallas as pl
from jax.experimental.pallas import tpu as pltpu

F32 = jnp.float32
BF16 = jnp.bfloat16

RMS_EPS = 1e-5
LN_EPS = 1e-5
CHUNK = 64
POOL_WINDOWS = (2, 4, 8, 16)
CONV_WIDTH = 3
TOP_K = 2

LANES = 128
V7X_VMEM_BYTES = 64 * 1024 * 1024
VMEM_LIMIT = V7X_VMEM_BYTES * 7 // 8
HALO = 16
T_SSM = LANES
TOKEN_TILE = 512
EXPERT_TILE = 1024
EXPERT_SUB = 512
EXPERT_F_TILE = 512
FFN_F_TILE = 1024


def _const_spec(shape):
    nd = len(shape)
    return pl.BlockSpec(shape, lambda *_: (0,) * nd, pipeline_mode=pl.Buffered(1))


def _layer_spec(arr, layer):
    nd = arr.ndim - 1
    return pl.BlockSpec((None,) + arr.shape[1:], lambda *_: (layer,) + (0,) * nd,
                        pipeline_mode=pl.Buffered(1))


def _params(*sem):
    return pltpu.CompilerParams(dimension_semantics=sem, vmem_limit_bytes=VMEM_LIMIT)


def _rms(x, g):
    return x * lax.rsqrt(jnp.mean(x * x, axis=-1, keepdims=True) + RMS_EPS) * g


def _dot(a, b):
    return jnp.dot(a, b, preferred_element_type=F32)


def _static_loop(n, body):
    for j in range(n):
        body(j)


def _sum_refs(refs):
    total = refs[0][...]
    for r in refs[1:]:
        total = total + r[...]
    return total


def _even_mixer_kernel(*refs, tm, cdim, nres):
    res_refs = refs[:nres]
    g_ref, win_ref, cw_ref, pw_ref, ps_ref, wout_ref, o_ref, zc_ext, p_ext = refs[nres:]
    j = pl.program_id(1)

    @pl.when(j == 0)
    def _():
        zc_ext[0:HALO, :] = jnp.zeros((HALO, cdim), F32)
        p_ext[0:HALO, :] = jnp.zeros((HALO, cdim), F32)

    h = _sum_refs(res_refs)
    hn = _rms(h, g_ref[...]).astype(BF16)
    z = _dot(hn, win_ref[...])
    gb = z[:, 0:cdim]
    zc = z[:, cdim:2 * cdim] * z[:, 2 * cdim:3 * cdim]
    p = z[:, 3 * cdim:4 * cdim]
    zc_ext[HALO:HALO + tm, :] = zc
    p_ext[HALO:HALO + tm, :] = p

    y = cw_ref[CONV_WIDTH - 1:CONV_WIDTH, :] * zc
    for k in range(CONV_WIDTH - 1):
        sh = CONV_WIDTH - 1 - k
        y = y + cw_ref[k:k + 1, :] * zc_ext[HALO - sh:HALO - sh + tm, :]
    parts = [(gb * y).astype(BF16)]

    pos = (j * tm + 1 + lax.broadcasted_iota(jnp.int32, (tm, LANES), 0)).astype(F32)
    for gi, w in enumerate(POOL_WINDOWS):
        c0 = gi * LANES
        pg = p[:, c0:c0 + LANES]
        s = pg
        for k in range(1, w):
            s = s + p_ext[HALO - k:HALO - k + tm, c0:c0 + LANES]
        pooled = s / jnp.minimum(pos, float(w)) - pg
        mixed = _dot(pooled.astype(BF16), pw_ref[gi]) * ps_ref[:, c0:c0 + LANES]
        parts.append(mixed.astype(BF16))

    mix = jnp.concatenate(parts, axis=1)
    o_ref[...] = h + _dot(mix, wout_ref[...])

    zc_ext[0:HALO, :] = zc_ext[tm:tm + HALO, :]
    p_ext[0:HALO, :] = p_ext[tm:tm + HALO, :]


def _even_mixer(h_parts, g, w_in, conv_w, pool_w, pool_scale, w_out, *, layer, bsz, seq):
    n, d = h_parts[0].shape
    cdim = w_in.shape[2] // 4
    tm = min(TOKEN_TILE, seq)
    nj = seq // tm
    kern = functools.partial(_even_mixer_kernel, tm=tm, cdim=cdim, nres=len(h_parts))
    return pl.pallas_call(
        kern,
        out_shape=jax.ShapeDtypeStruct((n, d), F32),
        grid=(bsz, nj),
        in_specs=[pl.BlockSpec((tm, d), lambda b, j: (b * nj + j, 0)) for _ in h_parts] + [
            _layer_spec(g, layer),
            _layer_spec(w_in, layer),
            _layer_spec(conv_w, layer),
            _layer_spec(pool_w, layer),
            _layer_spec(pool_scale, layer),
            _layer_spec(w_out, layer),
        ],
        out_specs=pl.BlockSpec((tm, d), lambda b, j: (b * nj + j, 0)),
        scratch_shapes=[pltpu.VMEM((HALO + tm, cdim), F32), pltpu.VMEM((HALO + tm, cdim), F32)],
        compiler_params=_params("arbitrary", "arbitrary"),
        name="even_mixer",
    )(*h_parts, g, w_in, conv_w, pool_w, pool_scale, w_out)


def _ffn_kernel(h_ref, g_ref, w1_ref, w3_ref, w2_ref, o_ref, *, fdim):
    h = h_ref[...]
    hn = _rms(h, g_ref[...]).astype(BF16)
    acc = h
    for f0 in range(0, fdim, FFN_F_TILE):
        f1 = min(f0 + FFN_F_TILE, fdim)
        a = _dot(hn, w1_ref[:, f0:f1])
        b = _dot(hn, w3_ref[:, f0:f1])
        act = (a * jax.nn.sigmoid(a) * b).astype(BF16)
        acc = acc + _dot(act, w2_ref[f0:f1, :])
    o_ref[...] = acc


def _ffn(h, g, w1, w3, w2, *, layer):
    n, d = h.shape
    fdim = w1.shape[2]
    tm = min(TOKEN_TILE, n)
    return pl.pallas_call(
        functools.partial(_ffn_kernel, fdim=fdim),
        out_shape=jax.ShapeDtypeStruct((n, d), F32),
        grid=(n // tm,),
        in_specs=[
            pl.BlockSpec((tm, d), lambda i: (i, 0)),
            _layer_spec(g, layer),
            _layer_spec(w1, layer),
            _layer_spec(w3, layer),
            _layer_spec(w2, layer),
        ],
        out_specs=pl.BlockSpec((tm, d), lambda i: (i, 0)),
        compiler_params=_params("arbitrary"),
        name="dense_swiglu",
    )(h, g, w1, w3, w2)


def _odd_in_kernel(h_ref, g_ref, wt_ref, o_ref, *, nck):
    hn = _rms(h_ref[...], g_ref[...]).astype(BF16)
    zt = lax.dot_general(wt_ref[...], hn, (((1,), (1,)), ((), ())), preferred_element_type=F32)
    for c in range(nck):
        o_ref[c] = zt[:, c * LANES:(c + 1) * LANES]


def _odd_in(h, g, w_in_t, *, layer):
    n, d = h.shape
    cols = w_in_t.shape[1]
    tm = min(TOKEN_TILE, n)
    nck = tm // LANES
    return pl.pallas_call(
        functools.partial(_odd_in_kernel, nck=nck),
        out_shape=jax.ShapeDtypeStruct((n // LANES, cols, LANES), F32),
        grid=(n // tm,),
        in_specs=[
            pl.BlockSpec((tm, d), lambda i: (i, 0)),
            _layer_spec(g, layer),
            _layer_spec(w_in_t, layer),
        ],
        out_specs=pl.BlockSpec((nck, cols, LANES), lambda i: (i, 0, 0)),
        compiler_params=_params("arbitrary"),
        name="odd_in_proj",
    )(h, g, w_in_t)


def _s5_kernel(u_ref, kt_ref, ws_ref, wc_ref, coef_ref, y_ref, m_scr, *, hdim, cpb, n_steps):
    t = T_SSM
    row = lax.broadcasted_iota(jnp.int32, (t, t), 0)
    col = lax.broadcasted_iota(jnp.int32, (t, t), 1)
    causal = col >= row
    def expand_row_block(j):
        r0 = j * t if isinstance(j, int) else pl.multiple_of(j * t, t)
        for i in range(hdim):
            kb = jnp.broadcast_to(kt_ref[0, j, i:i + 1, :], (t, t))
            tz = pltpu.roll(kb, 0, 1, stride=1, stride_axis=0)
            m_scr[pl.ds(r0, t), i * t:(i + 1) * t] = jnp.where(causal, tz, 0.0).astype(BF16)

    _static_loop(hdim, expand_row_block)

    x = jnp.concatenate([u_ref[:, c, :] for c in range(hdim)], axis=1).astype(BF16)
    y = _dot(x, m_scr[...])
    st = _dot(x, ws_ref[0])
    half = st.shape[1] // 2
    rb = lax.broadcasted_iota(jnp.int32, st.shape, 0) & (cpb - 1)
    for k in range(n_steps):
        d = 1 << k
        sh = jnp.where(rb >= d, pltpu.roll(st, d, 0), 0.0)
        sw = pltpu.roll(sh, half, 1)
        st = st + coef_ref[0, k, 0:1, :] * sh + coef_ref[0, k, 1:2, :] * sw
    prev = jnp.where(rb >= 1, pltpu.roll(st, 1, 0), 0.0)
    y = y + _dot(prev.astype(BF16), wc_ref[0])
    for i in range(hdim):
        y_ref[:, i, :] = y[:, i * t:(i + 1) * t]


def _s5(zt3, kt, ws, wc, coef, *, cpb):
    nc = zt3.shape[0]
    g, hdim = kt.shape[0], kt.shape[1]
    n_steps = coef.shape[1]
    return pl.pallas_call(
        functools.partial(_s5_kernel, hdim=hdim, cpb=cpb, n_steps=n_steps),
        out_shape=jax.ShapeDtypeStruct((nc, g * hdim, T_SSM), F32),
        grid=(g,),
        in_specs=[
            pl.BlockSpec((nc, hdim, T_SSM), lambda i: (0, i, 0)),
            pl.BlockSpec((1,) + kt.shape[1:], lambda i: (i, 0, 0, 0)),
            pl.BlockSpec((1,) + ws.shape[1:], lambda i: (i, 0, 0)),
            pl.BlockSpec((1,) + wc.shape[1:], lambda i: (i, 0, 0)),
            pl.BlockSpec((1,) + coef.shape[1:], lambda i: (i, 0, 0, 0)),
        ],
        out_specs=pl.BlockSpec((nc, hdim, T_SSM), lambda i: (0, i, 0)),
        scratch_shapes=[pltpu.VMEM((hdim * T_SSM, hdim * T_SSM), BF16)],
        compiler_params=_params("arbitrary"),
        name="s5_chunked",
    )(zt3, kt, ws, wc, coef)


def _s5_tables(lam_re, lam_im, log_dt, b_re, b_im, c_re, c_im, *, cpb):
    hp = lax.Precision.HIGHEST
    g, p = lam_re.shape
    hdim = b_re.shape[2]
    t = T_SSM
    dt = jnp.exp(log_dt)[:, None]
    lr, li = lam_re, lam_im
    mag = jnp.exp(lr * dt)
    abar_r = mag * jnp.cos(li * dt)
    abar_i = mag * jnp.sin(li * dt)
    qr, qi = abar_r - 1.0, abar_i
    den = lr * lr + li * li
    fr = ((qr * lr + qi * li) / den)[..., None]
    fi = ((qi * lr - qr * li) / den)[..., None]
    bbar_r = fr * b_re - fi * b_im
    bbar_i = fr * b_im + fi * b_re

    def apow(k):
        kk = k[None, :, None]
        m = jnp.exp(kk * (lr * dt)[:, None, :])
        th = kk * (li * dt)[:, None, :]
        return m * jnp.cos(th), m * jnp.sin(th)

    lags = jnp.arange(t, dtype=F32)
    pr, pi = apow(lags)
    car = c_re[:, None] * pr[:, :, None, :] - c_im[:, None] * pi[:, :, None, :]
    cai = c_re[:, None] * pi[:, :, None, :] + c_im[:, None] * pr[:, :, None, :]
    kmat = (jnp.einsum('gtip,gpj->gtij', car, bbar_r, precision=hp)
            - jnp.einsum('gtip,gpj->gtij', cai, bbar_i, precision=hp))
    kt = kmat.transpose(0, 3, 2, 1)
    er, ei = pr[:, ::-1], pi[:, ::-1]
    bjr = bbar_r.transpose(0, 2, 1)[:, :, None, :]
    bji = bbar_i.transpose(0, 2, 1)[:, :, None, :]
    ws_r = er[:, None] * bjr - ei[:, None] * bji
    ws_i = er[:, None] * bji + ei[:, None] * bjr
    ws = jnp.concatenate([ws_r, ws_i], axis=-1).reshape(g, hdim * t, 2 * p).astype(BF16)
    p1r, p1i = apow(lags + 1.0)
    ccr = c_re[:, :, None, :] * p1r[:, None] - c_im[:, :, None, :] * p1i[:, None]
    cci = c_re[:, :, None, :] * p1i[:, None] + c_im[:, :, None, :] * p1r[:, None]
    wc = jnp.concatenate([ccr, -cci], axis=-1).reshape(g, hdim * t, 2 * p)
    wc = wc.transpose(0, 2, 1).astype(BF16)
    n_steps = max(1, int(math.log2(cpb)))
    sr, si = apow(float(t) * (2.0 ** jnp.arange(n_steps, dtype=F32)))
    coef = jnp.stack([jnp.concatenate([sr, sr], axis=-1),
                      jnp.concatenate([-si, si], axis=-1)], axis=2)
    return kt, ws, wc, coef


def _odd_tail_kernel(z_ref, y_ref, h_ref, d_ref, gw_ref, gb_ref, lng_ref, sw_ref, sb_ref,
                     wout_ref, g2_ref, rw_ref, o_ref, xr_ref,
                     *, nck, sdim, heads, n_experts):
    def cat(ref, r0, r1):
        return jnp.concatenate([ref[c, r0:r1, :] for c in range(nck)], axis=1)

    def rep(ref, r0=None, r1=None):
        v = ref[...] if r0 is None else ref[r0:r1, :]
        return jnp.concatenate([v] * nck, axis=1)

    ut = cat(z_ref, 0, sdim)
    yt = cat(y_ref, 0, sdim)
    gt = jax.nn.gelu(yt + rep(d_ref) * ut)
    glu = _dot(gw_ref[...], gt.astype(BF16)) + rep(gb_ref)
    parts = [gt * jax.nn.sigmoid(glu)]

    hd = sdim // heads
    for hh in range(heads):
        r0 = sdim + hh * hd
        up = jax.nn.gelu(cat(z_ref, r0, r0 + hd))
        vp = jax.nn.gelu(cat(z_ref, sdim + r0, sdim + r0 + hd))
        mu = jnp.mean(vp, axis=0, keepdims=True)
        vc = vp - mu
        var = jnp.mean(vc * vc, axis=0, keepdims=True)
        vn = vc * lax.rsqrt(var + LN_EPS) * rep(lng_ref, hh * hd, (hh + 1) * hd)
        stacked = jnp.concatenate([vn[:, c * LANES:(c + 1) * LANES] for c in range(nck)], axis=0)
        s = _dot(stacked.astype(BF16), sw_ref[hh]) + sb_ref[hh]
        s = jnp.concatenate([s[c * hd:(c + 1) * hd, :] for c in range(nck)], axis=1)
        parts.append(up * s)

    mix = jnp.concatenate(parts, axis=0).T.astype(BF16)
    h1 = h_ref[...] + _dot(mix, wout_ref[...])
    o_ref[...] = h1

    dm = h1.shape[1]
    xn = _rms(h1, g2_ref[...])
    xr_ref[:, 0:dm] = xn
    logits = jnp.dot(xn, rw_ref[...], precision=lax.Precision.HIGHEST, preferred_element_type=F32)
    lane = lax.broadcasted_iota(jnp.int32, logits.shape, 1)
    neg = jnp.float32(-jnp.inf)
    lg = jnp.where(lane < n_experts, logits, neg)
    m1 = jnp.max(lg, axis=1, keepdims=True)
    i1 = jnp.min(jnp.where(lg == m1, lane, LANES), axis=1, keepdims=True)
    lg2 = jnp.where(lane == i1, neg, lg)
    m2 = jnp.max(lg2, axis=1, keepdims=True)
    i2 = jnp.min(jnp.where(lg2 == m2, lane, LANES), axis=1, keepdims=True)
    e2 = jnp.exp(m2 - m1)
    w1 = 1.0 / (1.0 + e2)
    w2 = e2 / (1.0 + e2)
    route = (jnp.where(lane == i1, w1, 0.0) + jnp.where(lane == i2, w2, 0.0)
             + jnp.where(lane == n_experts, i1.astype(F32), 0.0)
             + jnp.where(lane == n_experts + 1, i2.astype(F32), 0.0))
    xr_ref[:, dm:dm + LANES] = route


def _odd_tail(zt3, yt3, h, d_b, glu_wt, glu_b_b, lng_b, sgu_wt, sgu_b, w_out, g2, router_pad,
              *, layer, heads, n_experts):
    n, dm = h.shape
    cols = zt3.shape[1]
    sdim = yt3.shape[1]
    tm = min(TOKEN_TILE, n)
    nck = tm // LANES
    kern = functools.partial(_odd_tail_kernel, nck=nck, sdim=sdim, heads=heads, n_experts=n_experts)
    return pl.pallas_call(
        kern,
        out_shape=(jax.ShapeDtypeStruct((n, dm), F32),
                   jax.ShapeDtypeStruct((n, dm + LANES), F32)),
        grid=(n // tm,),
        in_specs=[
            pl.BlockSpec((nck, cols, LANES), lambda i: (i, 0, 0)),
            pl.BlockSpec((nck, sdim, LANES), lambda i: (i, 0, 0)),
            pl.BlockSpec((tm, dm), lambda i: (i, 0)),
            _const_spec(d_b.shape),
            _layer_spec(glu_wt, layer),
            _const_spec(glu_b_b.shape),
            _const_spec(lng_b.shape),
            _const_spec(sgu_wt.shape),
            _const_spec(sgu_b.shape),
            _layer_spec(w_out, layer),
            _layer_spec(g2, layer),
            _const_spec(router_pad.shape),
        ],
        out_specs=(pl.BlockSpec((tm, dm), lambda i: (i, 0)),
                   pl.BlockSpec((tm, dm + LANES), lambda i: (i, 0))),
        compiler_params=_params("arbitrary"),
        name="odd_mixer_tail",
    )(zt3, yt3, h, d_b, glu_wt, glu_b_b, lng_b, sgu_wt, sgu_b, w_out, g2, router_pad)


def _experts_kernel(te_ref, ns_ref, na_ref, xr_ref, w1_ref, w3_ref, w2_ref, o_ref, *, dm, sub):
    i = pl.program_id(0)
    f = pl.program_id(1)
    nsub = ns_ref[i]

    @pl.when(f == 0)
    def _():
        o_ref[...] = jnp.zeros(o_ref.shape, F32)

    for k in range(1, xr_ref.shape[0] // sub + 1):
        nrow = k * sub

        @pl.when(nsub == k)
        def _():
            x = xr_ref[0:nrow, 0:dm].astype(BF16)
            a = _dot(x, w1_ref[0].astype(BF16))
            b = _dot(x, w3_ref[0].astype(BF16))
            act = (a * jax.nn.sigmoid(a) * b).astype(BF16)
            o_ref[0:nrow, :] += _dot(act, w2_ref[0].astype(BF16))

    @pl.when(f == pl.num_programs(1) - 1)
    def _():
        routing = xr_ref[:, dm:dm + LANES]
        lane = lax.broadcasted_iota(jnp.int32, routing.shape, 1)
        gate = jnp.sum(jnp.where(lane == te_ref[i], routing, 0.0), axis=1, keepdims=True)
        o_ref[...] = o_ref[...] * gate


def _experts(tile_expert, tile_nsub, n_active, xrs, w1, w3, w2, *, layer):
    r = xrs.shape[0]
    d, fdim = w1.shape[2], w1.shape[3]
    te = EXPERT_TILE
    tf = min(EXPERT_F_TILE, fdim)
    n_tiles = r // te
    nf = fdim // tf

    def row_map(i, f, te_ref, ns_ref, na_ref):
        return (jnp.minimum(i, na_ref[0] - 1), 0)

    grid_spec = pltpu.PrefetchScalarGridSpec(
        num_scalar_prefetch=3,
        grid=(n_tiles, nf),
        in_specs=[
            pl.BlockSpec((te, d + LANES), row_map),
            pl.BlockSpec((None, 1, d, tf), lambda i, f, te_ref, *_: (layer, te_ref[i], 0, f)),
            pl.BlockSpec((None, 1, d, tf), lambda i, f, te_ref, *_: (layer, te_ref[i], 0, f)),
            pl.BlockSpec((None, 1, tf, d), lambda i, f, te_ref, *_: (layer, te_ref[i], f, 0)),
        ],
        out_specs=pl.BlockSpec((te, d), lambda i, f, *_: (i, 0)),
    )
    return pl.pallas_call(
        functools.partial(_experts_kernel, dm=d, sub=min(EXPERT_SUB, te)),
        out_shape=jax.ShapeDtypeStruct((r, d), F32),
        grid_spec=grid_spec,
        compiler_params=_params("arbitrary", "arbitrary"),
        name="grouped_experts",
    )(tile_expert, tile_nsub, n_active, xrs, w1, w3, w2)


def _dispatch_kernel(te_ref, off_ref, cb_ref, ct_ref, o_ref, *, te, nb, n):
    i = pl.program_id(0)
    e = te_ref[i]
    rank = i * te + lax.broadcasted_iota(jnp.int32, (1, te), 1) - off_ref[e]
    blk_end = cb_ref[0][:, LANES - 1:LANES]
    blk = jnp.sum((blk_end <= rank).astype(jnp.int32), axis=0, keepdims=True)
    blk = jnp.minimum(blk, nb - 1)
    onehot = (lax.broadcasted_iota(jnp.int32, (nb, te), 0) == blk).astype(F32)
    counts_t = jnp.dot(ct_ref[0].astype(F32), onehot, precision=lax.Precision.HIGHEST,
                       preferred_element_type=F32)
    within = jnp.sum((counts_t <= rank.astype(F32)).astype(jnp.int32), axis=0, keepdims=True)
    o_ref[0] = jnp.minimum(blk * LANES + within, n - 1)


def _dispatch_rows(tile_expert, offs, cb, ct, *, te, n):
    n_exp, nb, _ = cb.shape
    n_tiles = tile_expert.shape[0]
    grid_spec = pltpu.PrefetchScalarGridSpec(
        num_scalar_prefetch=2,
        grid=(n_tiles,),
        in_specs=[
            pl.BlockSpec((1, nb, LANES), lambda i, te_ref, off_ref: (te_ref[i], 0, 0)),
            pl.BlockSpec((1, LANES, nb), lambda i, te_ref, off_ref: (te_ref[i], 0, 0)),
        ],
        out_specs=pl.BlockSpec((1, 1, te), lambda i, te_ref, off_ref: (i, 0, 0)),
    )
    return pl.pallas_call(
        functools.partial(_dispatch_kernel, te=te, nb=nb, n=n),
        out_shape=jax.ShapeDtypeStruct((n_tiles, 1, te), jnp.int32),
        grid_spec=grid_spec,
        compiler_params=_params("arbitrary"),
        name="dispatch_rows",
    )(tile_expert, offs, cb, ct)


def _moe(h1, xr, w1, w3, w2, *, layer):
    n, d = h1.shape
    n_exp = w1.shape[1]
    te = EXPERT_TILE
    sub = min(EXPERT_SUB, te)
    n_tiles = (n * TOP_K) // te + n_exp
    nb = n // LANES

    ids = xr[:, d + n_exp:d + n_exp + TOP_K].astype(jnp.int32)
    sel = (ids[:, :, None] == jnp.arange(n_exp, dtype=jnp.int32)[None, None, :]).any(axis=1)
    csum = jnp.cumsum(sel.astype(jnp.int32), axis=0)
    counts = csum[-1]
    padded = ((counts + te - 1) // te) * te
    ends = jnp.cumsum(padded)
    offs = ends - padded
    tile_start = jnp.arange(n_tiles, dtype=jnp.int32) * te
    in_use = tile_start < ends[-1]
    n_active = (ends[-1] // te).astype(jnp.int32)
    tile_expert = jnp.sum(tile_start[:, None] >= ends[None, :], axis=1).astype(jnp.int32)
    last_expert = jnp.take(tile_expert, jnp.maximum(n_active - 1, 0))
    tile_expert = jnp.where(in_use, tile_expert, last_expert)
    valid = jnp.clip(counts[tile_expert] - (tile_start - offs[tile_expert]), 0, te)
    tile_nsub = jnp.where(in_use, (valid + sub - 1) // sub, 0).astype(jnp.int32)

    cb = csum.T.reshape(n_exp, nb, LANES)
    src = _dispatch_rows(tile_expert, offs.astype(jnp.int32), cb, cb.transpose(0, 2, 1), te=te, n=n)
    xrs = jnp.take(xr, src.reshape(-1), axis=0, mode='clip')
    ys = _experts(tile_expert, tile_nsub, n_active.reshape(1), xrs, w1, w3, w2, layer=layer)
    dest = offs[None, :] + csum - 1
    pos = jnp.take_along_axis(dest, ids, axis=1)
    return (h1, jnp.take(ys, pos[:, 0], axis=0, mode='clip'),
            jnp.take(ys, pos[:, 1], axis=0, mode='clip'))


def _final_norm_kernel(*refs):
    *res_refs, g_ref, o_ref = refs
    o_ref[...] = _rms(_sum_refs(res_refs), g_ref[...])


def _final_norm(h_parts, g):
    n, d = h_parts[0].shape
    tm = min(TOKEN_TILE, n)
    return pl.pallas_call(
        _final_norm_kernel,
        out_shape=jax.ShapeDtypeStruct((n, d), F32),
        grid=(n // tm,),
        in_specs=[pl.BlockSpec((tm, d), lambda i: (i, 0)) for _ in h_parts] + [_const_spec((1, d))],
        out_specs=pl.BlockSpec((tm, d), lambda i: (i, 0)),
        compiler_params=_params("arbitrary"),
        name="final_norm",
    )(*h_parts, g)


def _even_layer(h_parts, p, i, *, bsz, seq):
    h = _even_mixer(h_parts, p['ev_norm1_g'], p['ev_w_in'], p['ev_conv_w'], p['ev_pool_w'],
                    p['ev_pool_scale'], p['ev_w_out'], layer=i, bsz=bsz, seq=seq)
    return _ffn(h, p['ev_norm2_g'], p['ev_ffn_w1'], p['ev_ffn_w3'], p['ev_ffn_w2'], layer=i)


def _odd_layer(h, p, i, *, seq):
    sgu_w = p['od_sgu_w'][i]
    router_w = p['od_router_w'][i]
    heads, slen, _ = sgu_w.shape
    n_exp = router_w.shape[1]
    assert slen == LANES and seq % T_SSM == 0
    cpb = seq // T_SSM
    assert cpb & (cpb - 1) == 0, "chunks per sequence must be a power of two"

    zt3 = _odd_in(h, p['od_norm1_g'], p['od_w_in_t'], layer=i)
    kt, ws, wc, coef = _s5_tables(p['od_lambda_re'][i], p['od_lambda_im'][i], p['od_log_dt'][i],
                                  p['od_b_re'][i], p['od_b_im'][i], p['od_c_re'][i],
                                  p['od_c_im'][i], cpb=cpb)
    yt3 = _s5(zt3, kt, ws, wc, coef, cpb=cpb)

    lanes_b = lambda v: jnp.broadcast_to(v[:, None], (v.shape[0], LANES))
    cidx = jnp.arange(slen) // CHUNK
    mask = cidx[None, :] <= cidx[:, None]
    sgu_wt = jnp.where(mask[None], sgu_w, 0.0).transpose(0, 2, 1).astype(BF16)
    router_pad = jnp.zeros((router_w.shape[0], LANES), F32).at[:, :n_exp].set(router_w)
    h1, xr = _odd_tail(
        zt3, yt3, h, lanes_b(p['od_d'][i]), p['od_glu_wt'], lanes_b(p['od_glu_b'][i]),
        lanes_b(p['od_sgu_ln_g'][i]), sgu_wt, p['od_sgu_b'][i][:, None, :], p['od_w_out'],
        p['od_norm2_g'], router_pad, layer=i, heads=heads, n_experts=n_exp)
    return _moe(h1, xr, p['od_moe_w1'], p['od_moe_w3'], p['od_moe_w2'], layer=i)


def kernel(x, ev_norm1_g, ev_w_in, ev_conv_w, ev_pool_w, ev_pool_scale, ev_w_out, ev_norm2_g, ev_ffn_w1, ev_ffn_w3, ev_ffn_w2, od_norm1_g, od_w_in, od_lambda_re, od_lambda_im, od_log_dt, od_b_re, od_b_im, od_c_re, od_c_im, od_d, od_glu_w, od_glu_b, od_sgu_ln_g, od_sgu_w, od_sgu_b, od_w_out, od_norm2_g, od_router_w, od_moe_w1, od_moe_w3, od_moe_w2, final_norm_g):
    bsz, seq, d = x.shape
    depth = ev_norm1_g.shape[0] + od_norm1_g.shape[0]
    row = lambda v: v[:, None, :]
    p = dict(
        ev_norm1_g=row(ev_norm1_g), ev_w_in=ev_w_in.astype(BF16), ev_conv_w=ev_conv_w,
        ev_pool_w=ev_pool_w.astype(BF16), ev_pool_scale=row(ev_pool_scale),
        ev_w_out=ev_w_out.astype(BF16), ev_norm2_g=row(ev_norm2_g),
        ev_ffn_w1=ev_ffn_w1.astype(BF16), ev_ffn_w3=ev_ffn_w3.astype(BF16),
        ev_ffn_w2=ev_ffn_w2.astype(BF16),
        od_norm1_g=row(od_norm1_g), od_w_in_t=od_w_in.transpose(0, 2, 1).astype(BF16),
        od_lambda_re=od_lambda_re, od_lambda_im=od_lambda_im, od_log_dt=od_log_dt,
        od_b_re=od_b_re, od_b_im=od_b_im, od_c_re=od_c_re, od_c_im=od_c_im, od_d=od_d,
        od_glu_wt=od_glu_w.transpose(0, 2, 1).astype(BF16), od_glu_b=od_glu_b,
        od_sgu_ln_g=od_sgu_ln_g, od_sgu_w=od_sgu_w, od_sgu_b=od_sgu_b,
        od_w_out=od_w_out.astype(BF16), od_norm2_g=row(od_norm2_g), od_router_w=od_router_w,
        od_moe_w1=od_moe_w1, od_moe_w3=od_moe_w3, od_moe_w2=od_moe_w2,
    )
    h_parts = (x.reshape(bsz * seq, d),)
    for layer in range(depth):
        if layer % 2 == 0:
            h_parts = (_even_layer(h_parts, p, layer // 2, bsz=bsz, seq=seq),)
        else:
            assert len(h_parts) == 1
            h_parts = _odd_layer(h_parts[0], p, layer // 2, seq=seq)
    return _final_norm(h_parts, final_norm_g[None]).reshape(bsz, seq, d)
```

```python
import functools
import math

import jax
import jax.numpy as jnp
from jax import lax
from jax.experimental import pallas as pl
from jax.experimental.pallas import tpu as pltpu

F32 = jnp.float32
BF16 = jnp.bfloat16

RMS_EPS = 1e-5
LN_EPS = 1e-5
CHUNK = 64
POOL_WINDOWS = (2, 4, 8, 16)
CONV_WIDTH = 3
TOP_K = 2

LANES = 128
V7X_VMEM_BYTES = 64 * 1024 * 1024
VMEM_LIMIT = V7X_VMEM_BYTES * 7 // 8
HALO = 16
T_SSM = LANES
TOKEN_TILE = 512
ID_LANE = LANES - 4
GATE_LANE = LANES - 2
EXPERT_TILE = 2048
EXPERT_SUB = 512
EXPERT_F_TILE = 512
FFN_F_TILE = 1024


def _const_spec(shape):
    nd = len(shape)
    return pl.BlockSpec(shape, lambda *_: (0,) * nd, pipeline_mode=pl.Buffered(1))


def _layer_spec(arr, layer):
    nd = arr.ndim - 1
    return pl.BlockSpec((None,) + arr.shape[1:], lambda *_: (layer,) + (0,) * nd,
                        pipeline_mode=pl.Buffered(1))


def _params(*sem):
    return pltpu.CompilerParams(dimension_semantics=sem, vmem_limit_bytes=VMEM_LIMIT)


def _rms(x, g):
    return x * lax.rsqrt(jnp.mean(x * x, axis=-1, keepdims=True) + RMS_EPS) * g


def _dot(a, b):
    return jnp.dot(a, b, preferred_element_type=F32)


def _static_loop(n, body):
    for j in range(n):
        body(j)


def _residual(refs):
    if len(refs) == 1:
        return refs[0][...]
    h_ref, ya_ref, yb_ref, route_ref = refs
    route = route_ref[...]
    ga = route[:, GATE_LANE:GATE_LANE + 1]
    gb = route[:, GATE_LANE + 1:GATE_LANE + 2]
    return h_ref[...] + ga * ya_ref[...] + gb * yb_ref[...]


def _residual_specs(h_parts, tm, index_map):
    return [pl.BlockSpec((tm, p.shape[1]), index_map) for p in h_parts]


def _even_mixer_kernel(*refs, tm, cdim, nres):
    res_refs = refs[:nres]
    g_ref, win_ref, cw_ref, pw_ref, ps_ref, wout_ref, o_ref, zc_ext, p_ext = refs[nres:]
    j = pl.program_id(1)

    @pl.when(j == 0)
    def _():
        zc_ext[0:HALO, :] = jnp.zeros((HALO, cdim), F32)
        p_ext[0:HALO, :] = jnp.zeros((HALO, cdim), F32)

    h = _residual(res_refs)
    hn = _rms(h, g_ref[...]).astype(BF16)
    z = _dot(hn, win_ref[...])
    gb = z[:, 0:cdim]
    zc = z[:, cdim:2 * cdim] * z[:, 2 * cdim:3 * cdim]
    p = z[:, 3 * cdim:4 * cdim]
    zc_ext[HALO:HALO + tm, :] = zc
    p_ext[HALO:HALO + tm, :] = p

    y = cw_ref[CONV_WIDTH - 1:CONV_WIDTH, :] * zc
    for k in range(CONV_WIDTH - 1):
        sh = CONV_WIDTH - 1 - k
        y = y + cw_ref[k:k + 1, :] * zc_ext[HALO - sh:HALO - sh + tm, :]
    parts = [(gb * y).astype(BF16)]

    pos = (j * tm + 1 + lax.broadcasted_iota(jnp.int32, (tm, LANES), 0)).astype(F32)
    for gi, w in enumerate(POOL_WINDOWS):
        c0 = gi * LANES
        pg = p[:, c0:c0 + LANES]
        s = pg
        for k in range(1, w):
            s = s + p_ext[HALO - k:HALO - k + tm, c0:c0 + LANES]
        pooled = s / jnp.minimum(pos, float(w)) - pg
        mixed = _dot(pooled.astype(BF16), pw_ref[gi]) * ps_ref[:, c0:c0 + LANES]
        parts.append(mixed.astype(BF16))

    mix = jnp.concatenate(parts, axis=1)
    o_ref[...] = h + _dot(mix, wout_ref[...])

    zc_ext[0:HALO, :] = zc_ext[tm:tm + HALO, :]
    p_ext[0:HALO, :] = p_ext[tm:tm + HALO, :]


def _even_mixer(h_parts, g, w_in, conv_w, pool_w, pool_scale, w_out, *, layer, bsz, seq):
    n, d = h_parts[0].shape
    cdim = w_in.shape[2] // 4
    tm = min(TOKEN_TILE, seq)
    nj = seq // tm
    kern = functools.partial(_even_mixer_kernel, tm=tm, cdim=cdim, nres=len(h_parts))
    return pl.pallas_call(
        kern,
        out_shape=jax.ShapeDtypeStruct((n, d), F32),
        grid=(bsz, nj),
        in_specs=_residual_specs(h_parts, tm, lambda b, j: (b * nj + j, 0)) + [
            _layer_spec(g, layer),
            _layer_spec(w_in, layer),
            _layer_spec(conv_w, layer),
            _layer_spec(pool_w, layer),
            _layer_spec(pool_scale, layer),
            _layer_spec(w_out, layer),
        ],
        out_specs=pl.BlockSpec((tm, d), lambda b, j: (b * nj + j, 0)),
        scratch_shapes=[pltpu.VMEM((HALO + tm, cdim), F32), pltpu.VMEM((HALO + tm, cdim), F32)],
        compiler_params=_params("arbitrary", "arbitrary"),
        name="even_mixer",
    )(*h_parts, g, w_in, conv_w, pool_w, pool_scale, w_out)


def _ffn_kernel(h_ref, g_ref, w1_ref, w3_ref, w2_ref, o_ref, *, fdim):
    h = h_ref[...]
    hn = _rms(h, g_ref[...]).astype(BF16)
    acc = h
    for f0 in range(0, fdim, FFN_F_TILE):
        f1 = min(f0 + FFN_F_TILE, fdim)
        a = _dot(hn, w1_ref[:, f0:f1])
        b = _dot(hn, w3_ref[:, f0:f1])
        act = (a * jax.nn.sigmoid(a) * b).astype(BF16)
        acc = acc + _dot(act, w2_ref[f0:f1, :])
    o_ref[...] = acc


def _ffn(h, g, w1, w3, w2, *, layer):
    n, d = h.shape
    fdim = w1.shape[2]
    tm = min(TOKEN_TILE, n)
    return pl.pallas_call(
        functools.partial(_ffn_kernel, fdim=fdim),
        out_shape=jax.ShapeDtypeStruct((n, d), F32),
        grid=(n // tm,),
        in_specs=[
            pl.BlockSpec((tm, d), lambda i: (i, 0)),
            _layer_spec(g, layer),
            _layer_spec(w1, layer),
            _layer_spec(w3, layer),
            _layer_spec(w2, layer),
        ],
        out_specs=pl.BlockSpec((tm, d), lambda i: (i, 0)),
        compiler_params=_params("arbitrary"),
        name="dense_swiglu",
    )(h, g, w1, w3, w2)


def _odd_in_kernel(h_ref, g_ref, wt_ref, o_ref, *, nck):
    hn = _rms(h_ref[...], g_ref[...]).astype(BF16)
    zt = lax.dot_general(wt_ref[...], hn, (((1,), (1,)), ((), ())), preferred_element_type=F32)
    for c in range(nck):
        o_ref[c] = zt[:, c * LANES:(c + 1) * LANES]


def _odd_in(h, g, w_in_t, *, layer):
    n, d = h.shape
    cols = w_in_t.shape[1]
    tm = min(TOKEN_TILE, n)
    nck = tm // LANES
    return pl.pallas_call(
        functools.partial(_odd_in_kernel, nck=nck),
        out_shape=jax.ShapeDtypeStruct((n // LANES, cols, LANES), F32),
        grid=(n // tm,),
        in_specs=[
            pl.BlockSpec((tm, d), lambda i: (i, 0)),
            _layer_spec(g, layer),
            _layer_spec(w_in_t, layer),
        ],
        out_specs=pl.BlockSpec((nck, cols, LANES), lambda i: (i, 0, 0)),
        compiler_params=_params("arbitrary"),
        name="odd_in_proj",
    )(h, g, w_in_t)


def _s5_kernel(u_ref, kt_ref, ws_ref, wc_ref, coef_ref, y_ref, m_scr, *, hdim, cpb, n_steps):
    t = T_SSM
    row = lax.broadcasted_iota(jnp.int32, (t, t), 0)
    col = lax.broadcasted_iota(jnp.int32, (t, t), 1)
    causal = col >= row
    def expand_row_block(j):
        r0 = j * t if isinstance(j, int) else pl.multiple_of(j * t, t)
        for i in range(hdim):
            kb = jnp.broadcast_to(kt_ref[0, j, i:i + 1, :], (t, t))
            tz = pltpu.roll(kb, 0, 1, stride=1, stride_axis=0)
            m_scr[pl.ds(r0, t), i * t:(i + 1) * t] = jnp.where(causal, tz, 0.0).astype(BF16)

    _static_loop(hdim, expand_row_block)

    x = jnp.concatenate([u_ref[:, c, :] for c in range(hdim)], axis=1).astype(BF16)
    y = _dot(x, m_scr[...])
    st = _dot(x, ws_ref[0])
    half = st.shape[1] // 2
    rb = lax.broadcasted_iota(jnp.int32, st.shape, 0) & (cpb - 1)
    for k in range(n_steps):
        d = 1 << k
        sh = jnp.where(rb >= d, pltpu.roll(st, d, 0), 0.0)
        sw = pltpu.roll(sh, half, 1)
        st = st + coef_ref[0, k, 0:1, :] * sh + coef_ref[0, k, 1:2, :] * sw
    prev = jnp.where(rb >= 1, pltpu.roll(st, 1, 0), 0.0)
    y = y + _dot(prev.astype(BF16), wc_ref[0])
    for i in range(hdim):
        y_ref[:, i, :] = y[:, i * t:(i + 1) * t]


def _s5(zt3, kt, ws, wc, coef, *, cpb):
    nc = zt3.shape[0]
    g, hdim = kt.shape[0], kt.shape[1]
    n_steps = coef.shape[1]
    return pl.pallas_call(
        functools.partial(_s5_kernel, hdim=hdim, cpb=cpb, n_steps=n_steps),
        out_shape=jax.ShapeDtypeStruct((nc, g * hdim, T_SSM), F32),
        grid=(g,),
        in_specs=[
            pl.BlockSpec((nc, hdim, T_SSM), lambda i: (0, i, 0)),
            pl.BlockSpec((1,) + kt.shape[1:], lambda i: (i, 0, 0, 0)),
            pl.BlockSpec((1,) + ws.shape[1:], lambda i: (i, 0, 0)),
            pl.BlockSpec((1,) + wc.shape[1:], lambda i: (i, 0, 0)),
            pl.BlockSpec((1,) + coef.shape[1:], lambda i: (i, 0, 0, 0)),
        ],
        out_specs=pl.BlockSpec((nc, hdim, T_SSM), lambda i: (0, i, 0)),
        scratch_shapes=[pltpu.VMEM((hdim * T_SSM, hdim * T_SSM), BF16)],
        compiler_params=_params("arbitrary"),
        name="s5_chunked",
    )(zt3, kt, ws, wc, coef)


def _s5_tables(lam_re, lam_im, log_dt, b_re, b_im, c_re, c_im, *, cpb):
    hp = lax.Precision.HIGHEST
    g, p = lam_re.shape
    hdim = b_re.shape[2]
    t = T_SSM
    dt = jnp.exp(log_dt)[:, None]
    lr, li = lam_re, lam_im
    mag = jnp.exp(lr * dt)
    abar_r = mag * jnp.cos(li * dt)
    abar_i = mag * jnp.sin(li * dt)
    qr, qi = abar_r - 1.0, abar_i
    den = lr * lr + li * li
    fr = ((qr * lr + qi * li) / den)[..., None]
    fi = ((qi * lr - qr * li) / den)[..., None]
    bbar_r = fr * b_re - fi * b_im
    bbar_i = fr * b_im + fi * b_re

    def apow(k):
        kk = k[None, :, None]
        m = jnp.exp(kk * (lr * dt)[:, None, :])
        th = kk * (li * dt)[:, None, :]
        return m * jnp.cos(th), m * jnp.sin(th)

    lags = jnp.arange(t, dtype=F32)
    pr, pi = apow(lags)
    car = c_re[:, None] * pr[:, :, None, :] - c_im[:, None] * pi[:, :, None, :]
    cai = c_re[:, None] * pi[:, :, None, :] + c_im[:, None] * pr[:, :, None, :]
    kmat = (jnp.einsum('gtip,gpj->gtij', car, bbar_r, precision=hp)
            - jnp.einsum('gtip,gpj->gtij', cai, bbar_i, precision=hp))
    kt = kmat.transpose(0, 3, 2, 1)
    er, ei = pr[:, ::-1], pi[:, ::-1]
    bjr = bbar_r.transpose(0, 2, 1)[:, :, None, :]
    bji = bbar_i.transpose(0, 2, 1)[:, :, None, :]
    ws_r = er[:, None] * bjr - ei[:, None] * bji
    ws_i = er[:, None] * bji + ei[:, None] * bjr
    ws = jnp.concatenate([ws_r, ws_i], axis=-1).reshape(g, hdim * t, 2 * p).astype(BF16)
    p1r, p1i = apow(lags + 1.0)
    ccr = c_re[:, :, None, :] * p1r[:, None] - c_im[:, :, None, :] * p1i[:, None]
    cci = c_re[:, :, None, :] * p1i[:, None] + c_im[:, :, None, :] * p1r[:, None]
    wc = jnp.concatenate([ccr, -cci], axis=-1).reshape(g, hdim * t, 2 * p)
    wc = wc.transpose(0, 2, 1).astype(BF16)
    n_steps = max(1, int(math.log2(cpb)))
    sr, si = apow(float(t) * (2.0 ** jnp.arange(n_steps, dtype=F32)))
    coef = jnp.stack([jnp.concatenate([sr, sr], axis=-1),
                      jnp.concatenate([-si, si], axis=-1)], axis=2)
    return kt, ws, wc, coef


def _odd_tail_kernel(z_ref, y_ref, h_ref, d_ref, gw_ref, gb_ref, lng_ref, sw_ref, sb_ref,
                     wout_ref, g2_ref, rw_ref, o_ref, xn_ref, route_ref,
                     *, nck, sdim, heads, n_experts):
    def cat(ref, r0, r1):
        return jnp.concatenate([ref[c, r0:r1, :] for c in range(nck)], axis=1)

    def rep(ref, r0=None, r1=None):
        v = ref[...] if r0 is None else ref[r0:r1, :]
        return jnp.concatenate([v] * nck, axis=1)

    ut = cat(z_ref, 0, sdim)
    yt = cat(y_ref, 0, sdim)
    gt = jax.nn.gelu(yt + rep(d_ref) * ut)
    glu = _dot(gw_ref[...], gt.astype(BF16)) + rep(gb_ref)
    parts = [gt * jax.nn.sigmoid(glu)]

    hd = sdim // heads
    for hh in range(heads):
        r0 = sdim + hh * hd
        up = jax.nn.gelu(cat(z_ref, r0, r0 + hd))
        vp = jax.nn.gelu(cat(z_ref, sdim + r0, sdim + r0 + hd))
        mu = jnp.mean(vp, axis=0, keepdims=True)
        vc = vp - mu
        var = jnp.mean(vc * vc, axis=0, keepdims=True)
        vn = vc * lax.rsqrt(var + LN_EPS) * rep(lng_ref, hh * hd, (hh + 1) * hd)
        stacked = jnp.concatenate([vn[:, c * LANES:(c + 1) * LANES] for c in range(nck)], axis=0)
        s = _dot(stacked.astype(BF16), sw_ref[hh]) + sb_ref[hh]
        s = jnp.concatenate([s[c * hd:(c + 1) * hd, :] for c in range(nck)], axis=1)
        parts.append(up * s)

    mix = jnp.concatenate(parts, axis=0).T.astype(BF16)
    h1 = h_ref[...] + _dot(mix, wout_ref[...])
    o_ref[...] = h1

    xn = _rms(h1, g2_ref[...])
    xn_ref[...] = xn
    logits = jnp.dot(xn, rw_ref[...], precision=lax.Precision.HIGHEST, preferred_element_type=F32)
    lane = lax.broadcasted_iota(jnp.int32, logits.shape, 1)
    neg = jnp.float32(-jnp.inf)
    lg = jnp.where(lane < n_experts, logits, neg)
    m1 = jnp.max(lg, axis=1, keepdims=True)
    i1 = jnp.min(jnp.where(lg == m1, lane, LANES), axis=1, keepdims=True)
    lg2 = jnp.where(lane == i1, neg, lg)
    m2 = jnp.max(lg2, axis=1, keepdims=True)
    i2 = jnp.min(jnp.where(lg2 == m2, lane, LANES), axis=1, keepdims=True)
    e2 = jnp.exp(m2 - m1)
    w1 = 1.0 / (1.0 + e2)
    w2 = e2 / (1.0 + e2)
    route_ref[...] = (jnp.where(lane == ID_LANE, i1.astype(F32), 0.0)
                      + jnp.where(lane == ID_LANE + 1, i2.astype(F32), 0.0)
                      + jnp.where(lane == GATE_LANE, w1, 0.0)
                      + jnp.where(lane == GATE_LANE + 1, w2, 0.0))


def _odd_tail(zt3, yt3, h, d_b, glu_wt, glu_b_b, lng_b, sgu_wt, sgu_b, w_out, g2, router_pad,
              *, layer, heads, n_experts):
    n, dm = h.shape
    cols = zt3.shape[1]
    sdim = yt3.shape[1]
    tm = min(TOKEN_TILE, n)
    nck = tm // LANES
    kern = functools.partial(_odd_tail_kernel, nck=nck, sdim=sdim, heads=heads, n_experts=n_experts)
    return pl.pallas_call(
        kern,
        out_shape=(jax.ShapeDtypeStruct((n, dm), F32),
                   jax.ShapeDtypeStruct((n, dm), F32),
                   jax.ShapeDtypeStruct((n, LANES), F32)),
        grid=(n // tm,),
        in_specs=[
            pl.BlockSpec((nck, cols, LANES), lambda i: (i, 0, 0)),
            pl.BlockSpec((nck, sdim, LANES), lambda i: (i, 0, 0)),
            pl.BlockSpec((tm, dm), lambda i: (i, 0)),
            _const_spec(d_b.shape),
            _layer_spec(glu_wt, layer),
            _const_spec(glu_b_b.shape),
            _const_spec(lng_b.shape),
            _const_spec(sgu_wt.shape),
            _const_spec(sgu_b.shape),
            _layer_spec(w_out, layer),
            _layer_spec(g2, layer),
            _const_spec(router_pad.shape),
        ],
        out_specs=(pl.BlockSpec((tm, dm), lambda i: (i, 0)),
                   pl.BlockSpec((tm, dm), lambda i: (i, 0)),
                   pl.BlockSpec((tm, LANES), lambda i: (i, 0))),
        compiler_params=_params("arbitrary"),
        name="odd_mixer_tail",
    )(zt3, yt3, h, d_b, glu_wt, glu_b_b, lng_b, sgu_wt, sgu_b, w_out, g2, router_pad)


def _experts_kernel(te_ref, ns_ref, na_ref, x_ref, w1_ref, w3_ref, w2_ref, o_ref, *, sub):
    f = pl.program_id(1)

    @pl.when(f == 0)
    def _():
        o_ref[...] = jnp.zeros(o_ref.shape, F32)

    def sub_tile(s, carry):
        rows = pl.ds(pl.multiple_of(s * sub, sub), sub)
        x = x_ref[rows, :].astype(BF16)
        a = _dot(x, w1_ref[0].astype(BF16))
        b = _dot(x, w3_ref[0].astype(BF16))
        act = (a * jax.nn.sigmoid(a) * b).astype(BF16)
        o_ref[rows, :] += _dot(act, w2_ref[0].astype(BF16))
        return carry

    lax.fori_loop(0, ns_ref[pl.program_id(0)], sub_tile, 0)


def _experts(tile_expert, tile_nsub, n_active, xs, w1, w3, w2, *, layer):
    r, d = xs.shape
    fdim = w1.shape[3]
    te = EXPERT_TILE
    tf = min(EXPERT_F_TILE, fdim)
    n_tiles = r // te
    nf = fdim // tf

    def row_map(i, f, te_ref, ns_ref, na_ref):
        return (jnp.minimum(i, na_ref[0] - 1), 0)

    grid_spec = pltpu.PrefetchScalarGridSpec(
        num_scalar_prefetch=3,
        grid=(n_tiles, nf),
        in_specs=[
            pl.BlockSpec((te, d), row_map),
            pl.BlockSpec((None, 1, d, tf), lambda i, f, te_ref, *_: (layer, te_ref[i], 0, f)),
            pl.BlockSpec((None, 1, d, tf), lambda i, f, te_ref, *_: (layer, te_ref[i], 0, f)),
            pl.BlockSpec((None, 1, tf, d), lambda i, f, te_ref, *_: (layer, te_ref[i], f, 0)),
        ],
        out_specs=pl.BlockSpec((te, d), lambda i, f, *_: (i, 0)),
    )
    return pl.pallas_call(
        functools.partial(_experts_kernel, sub=min(EXPERT_SUB, te)),
        out_shape=jax.ShapeDtypeStruct((r, d), F32),
        grid_spec=grid_spec,
        compiler_params=_params("arbitrary", "arbitrary"),
        name="grouped_experts",
    )(tile_expert, tile_nsub, n_active, xs, w1, w3, w2)


def _dispatch_kernel(te_ref, off_ref, cnt_ref, cb_ref, ct_ref, o_ref, *, te, nb, n):
    i = pl.program_id(0)
    e = te_ref[i]
    row = i * te + lax.broadcasted_iota(jnp.int32, (1, te), 1)
    rank = row - off_ref[e]
    blk_end = cb_ref[0][:, LANES - 1:LANES]
    blk = jnp.sum((blk_end <= rank).astype(jnp.int32), axis=0, keepdims=True)
    blk = jnp.minimum(blk, nb - 1)
    onehot = (lax.broadcasted_iota(jnp.int32, (nb, te), 0) == blk).astype(F32)
    counts_t = jnp.dot(ct_ref[0].astype(F32), onehot, precision=lax.Precision.HIGHEST,
                       preferred_element_type=F32)
    within = jnp.sum((counts_t <= rank.astype(F32)).astype(jnp.int32), axis=0, keepdims=True)
    filler = row & (n - 1) if n & (n - 1) == 0 else lax.rem(row, n)
    o_ref[0] = jnp.where(rank < cnt_ref[e], blk * LANES + within, filler)


def _dispatch_rows(tile_expert, offs, counts, cb, ct, *, te, n):
    n_exp, nb, _ = cb.shape
    n_tiles = tile_expert.shape[0]
    grid_spec = pltpu.PrefetchScalarGridSpec(
        num_scalar_prefetch=3,
        grid=(n_tiles,),
        in_specs=[
            pl.BlockSpec((1, nb, LANES), lambda i, te_ref, *_: (te_ref[i], 0, 0)),
            pl.BlockSpec((1, LANES, nb), lambda i, te_ref, *_: (te_ref[i], 0, 0)),
        ],
        out_specs=pl.BlockSpec((1, 1, te), lambda i, *_: (i, 0, 0)),
    )
    return pl.pallas_call(
        functools.partial(_dispatch_kernel, te=te, nb=nb, n=n),
        out_shape=jax.ShapeDtypeStruct((n_tiles, 1, te), jnp.int32),
        grid_spec=grid_spec,
        compiler_params=_params("arbitrary"),
        name="dispatch_rows",
    )(tile_expert, offs, counts, cb, ct)


def _moe(h1, xn, route, w1, w3, w2, *, layer):
    n, d = h1.shape
    n_exp = w1.shape[1]
    te = EXPERT_TILE
    sub = min(EXPERT_SUB, te)
    n_tiles = (n * TOP_K + te - 1) // te + n_exp
    nb = n // LANES

    ids = route[:, ID_LANE:ID_LANE + TOP_K].astype(jnp.int32)
    sel = (ids[:, :, None] == jnp.arange(n_exp, dtype=jnp.int32)[None, None, :]).any(axis=1)
    csum = jnp.cumsum(sel.astype(jnp.int32), axis=0)
    counts = csum[-1]
    padded = ((counts + te - 1) // te) * te
    ends = jnp.cumsum(padded)
    offs = ends - padded
    tile_start = jnp.arange(n_tiles, dtype=jnp.int32) * te
    in_use = tile_start < ends[-1]
    n_active = (ends[-1] // te).astype(jnp.int32)
    tile_expert = jnp.sum(tile_start[:, None] >= ends[None, :], axis=1).astype(jnp.int32)
    last_expert = jnp.take(tile_expert, jnp.maximum(n_active - 1, 0))
    tile_expert = jnp.where(in_use, tile_expert, last_expert)
    valid = jnp.clip(counts[tile_expert] - (tile_start - offs[tile_expert]), 0, te)
    tile_nsub = jnp.where(in_use, (valid + sub - 1) // sub, 0).astype(jnp.int32)

    cb = csum.T.reshape(n_exp, nb, LANES)
    src = _dispatch_rows(tile_expert, offs.astype(jnp.int32), counts.astype(jnp.int32), cb,
                         cb.transpose(0, 2, 1), te=te, n=n)
    xs = jnp.take(xn, src.reshape(-1), axis=0, mode='clip')
    ys = _experts(tile_expert, tile_nsub, n_active.reshape(1), xs, w1, w3, w2, layer=layer)
    dest = offs[None, :] + csum - 1
    pos = jnp.take_along_axis(dest, ids, axis=1)
    return (h1, jnp.take(ys, pos[:, 0], axis=0, mode='clip'),
            jnp.take(ys, pos[:, 1], axis=0, mode='clip'), route)


def _final_norm_kernel(*refs):
    *res_refs, g_ref, o_ref = refs
    o_ref[...] = _rms(_residual(res_refs), g_ref[...])


def _final_norm(h_parts, g):
    n, d = h_parts[0].shape
    tm = min(TOKEN_TILE, n)
    return pl.pallas_call(
        _final_norm_kernel,
        out_shape=jax.ShapeDtypeStruct((n, d), F32),
        grid=(n // tm,),
        in_specs=_residual_specs(h_parts, tm, lambda i: (i, 0)) + [_const_spec((1, d))],
        out_specs=pl.BlockSpec((tm, d), lambda i: (i, 0)),
        compiler_params=_params("arbitrary"),
        name="final_norm",
    )(*h_parts, g)


def _even_layer(h_parts, p, i, *, bsz, seq):
    h = _even_mixer(h_parts, p['ev_norm1_g'], p['ev_w_in'], p['ev_conv_w'], p['ev_pool_w'],
                    p['ev_pool_scale'], p['ev_w_out'], layer=i, bsz=bsz, seq=seq)
    return _ffn(h, p['ev_norm2_g'], p['ev_ffn_w1'], p['ev_ffn_w3'], p['ev_ffn_w2'], layer=i)


def _odd_layer(h, p, i, *, seq):
    sgu_w = p['od_sgu_w'][i]
    router_w = p['od_router_w'][i]
    heads, slen, _ = sgu_w.shape
    n_exp = router_w.shape[1]
    assert slen == LANES and seq % T_SSM == 0
    cpb = seq // T_SSM
    assert cpb & (cpb - 1) == 0, "chunks per sequence must be a power of two"

    zt3 = _odd_in(h, p['od_norm1_g'], p['od_w_in_t'], layer=i)
    kt, ws, wc, coef = _s5_tables(p['od_lambda_re'][i], p['od_lambda_im'][i], p['od_log_dt'][i],
                                  p['od_b_re'][i], p['od_b_im'][i], p['od_c_re'][i],
                                  p['od_c_im'][i], cpb=cpb)
    yt3 = _s5(zt3, kt, ws, wc, coef, cpb=cpb)

    lanes_b = lambda v: jnp.broadcast_to(v[:, None], (v.shape[0], LANES))
    cidx = jnp.arange(slen) // CHUNK
    mask = cidx[None, :] <= cidx[:, None]
    sgu_wt = jnp.where(mask[None], sgu_w, 0.0).transpose(0, 2, 1).astype(BF16)
    router_pad = jnp.zeros((router_w.shape[0], LANES), F32).at[:, :n_exp].set(router_w)
    h1, xn, route = _odd_tail(
        zt3, yt3, h, lanes_b(p['od_d'][i]), p['od_glu_wt'], lanes_b(p['od_glu_b'][i]),
        lanes_b(p['od_sgu_ln_g'][i]), sgu_wt, p['od_sgu_b'][i][:, None, :], p['od_w_out'],
        p['od_norm2_g'], router_pad, layer=i, heads=heads, n_experts=n_exp)
    return _moe(h1, xn, route, p['od_moe_w1'], p['od_moe_w3'], p['od_moe_w2'], layer=i)


def kernel(x, ev_norm1_g, ev_w_in, ev_conv_w, ev_pool_w, ev_pool_scale, ev_w_out, ev_norm2_g, ev_ffn_w1, ev_ffn_w3, ev_ffn_w2, od_norm1_g, od_w_in, od_lambda_re, od_lambda_im, od_log_dt, od_b_re, od_b_im, od_c_re, od_c_im, od_d, od_glu_w, od_glu_b, od_sgu_ln_g, od_sgu_w, od_sgu_b, od_w_out, od_norm2_g, od_router_w, od_moe_w1, od_moe_w3, od_moe_w2, final_norm_g):
    bsz, seq, d = x.shape
    depth = ev_norm1_g.shape[0] + od_norm1_g.shape[0]
    row = lambda v: v[:, None, :]
    p = dict(
        ev_norm1_g=row(ev_norm1_g), ev_w_in=ev_w_in.astype(BF16), ev_conv_w=ev_conv_w,
        ev_pool_w=ev_pool_w.astype(BF16), ev_pool_scale=row(ev_pool_scale),
        ev_w_out=ev_w_out.astype(BF16), ev_norm2_g=row(ev_norm2_g),
        ev_ffn_w1=ev_ffn_w1.astype(BF16), ev_ffn_w3=ev_ffn_w3.astype(BF16),
        ev_ffn_w2=ev_ffn_w2.astype(BF16),
        od_norm1_g=row(od_norm1_g), od_w_in_t=od_w_in.transpose(0, 2, 1).astype(BF16),
        od_lambda_re=od_lambda_re, od_lambda_im=od_lambda_im, od_log_dt=od_log_dt,
        od_b_re=od_b_re, od_b_im=od_b_im, od_c_re=od_c_re, od_c_im=od_c_im, od_d=od_d,
        od_glu_wt=od_glu_w.transpose(0, 2, 1).astype(BF16), od_glu_b=od_glu_b,
        od_sgu_ln_g=od_sgu_ln_g, od_sgu_w=od_sgu_w, od_sgu_b=od_sgu_b,
        od_w_out=od_w_out.astype(BF16), od_norm2_g=row(od_norm2_g), od_router_w=od_router_w,
        od_moe_w1=od_moe_w1, od_moe_w3=od_moe_w3, od_moe_w2=od_moe_w2,
    )
    h_parts = (x.reshape(bsz * seq, d),)
    for layer in range(depth):
        if layer % 2 == 0:
            h_parts = (_even_layer(h_parts, p, layer // 2, bsz=bsz, seq=seq),)
        else:
            assert len(h_parts) == 1
            h_parts = _odd_layer(h_parts[0], p, layer // 2, seq=seq)
    return _final_norm(h_parts, final_norm_g[None]).reshape(bsz, seq, d)
```

```python
import functools
import math

import jax
import jax.numpy as jnp
from jax import lax
from jax.experimental import pallas as pl
from jax.experimental.pallas import tpu as pltpu

F32 = jnp.float32
BF16 = jnp.bfloat16

RMS_EPS = 1e-5
LN_EPS = 1e-5
CHUNK = 64
POOL_WINDOWS = (2, 4, 8, 16)
CONV_WIDTH = 3
TOP_K = 2

LANES = 128
V7X_VMEM_BYTES = 64 * 1024 * 1024
VMEM_LIMIT = V7X_VMEM_BYTES * 7 // 8
HALO = 16
T_SSM = LANES
TOKEN_TILE = 512
ID_LANE = LANES - 4
GATE_LANE = LANES - 2
EXPERT_TILE = 2048
EXPERT_SUB = 512
EXPERT_F_TILE = 512
FFN_F_TILE = 1024


def _const_spec(shape):
    nd = len(shape)
    return pl.BlockSpec(shape, lambda *_: (0,) * nd, pipeline_mode=pl.Buffered(1))


def _layer_spec(arr, layer):
    nd = arr.ndim - 1
    return pl.BlockSpec((None,) + arr.shape[1:], lambda *_: (layer,) + (0,) * nd,
                        pipeline_mode=pl.Buffered(1))


def _params(*sem):
    return pltpu.CompilerParams(dimension_semantics=sem, vmem_limit_bytes=VMEM_LIMIT)


def _rms(x, g):
    return x * lax.rsqrt(jnp.mean(x * x, axis=-1, keepdims=True) + RMS_EPS) * g


def _dot(a, b):
    return jnp.dot(a, b, preferred_element_type=F32)


def _static_loop(n, body):
    for j in range(n):
        body(j)


def _residual(refs):
    if len(refs) == 1:
        return refs[0][...]
    h_ref, ya_ref, yb_ref, route_ref = refs
    route = route_ref[...]
    ga = route[:, GATE_LANE:GATE_LANE + 1]
    gb = route[:, GATE_LANE + 1:GATE_LANE + 2]
    return h_ref[...] + ga * ya_ref[...] + gb * yb_ref[...]


def _residual_specs(h_parts, tm, index_map):
    return [pl.BlockSpec((tm, p.shape[1]), index_map) for p in h_parts]


def _even_mixer_kernel(*refs, tm, cdim, nres):
    res_refs = refs[:nres]
    g_ref, win_ref, cw_ref, pw_ref, ps_ref, wout_ref, o_ref, zc_ext, p_ext = refs[nres:]
    j = pl.program_id(1)

    @pl.when(j == 0)
    def _():
        zc_ext[0:HALO, :] = jnp.zeros((HALO, cdim), F32)
        p_ext[0:HALO, :] = jnp.zeros((HALO, cdim), F32)

    h = _residual(res_refs)
    hn = _rms(h, g_ref[...]).astype(BF16)
    z = _dot(hn, win_ref[...])
    gb = z[:, 0:cdim]
    zc = z[:, cdim:2 * cdim] * z[:, 2 * cdim:3 * cdim]
    p = z[:, 3 * cdim:4 * cdim]
    zc_ext[HALO:HALO + tm, :] = zc
    p_ext[HALO:HALO + tm, :] = p

    y = cw_ref[CONV_WIDTH - 1:CONV_WIDTH, :] * zc
    for k in range(CONV_WIDTH - 1):
        sh = CONV_WIDTH - 1 - k
        y = y + cw_ref[k:k + 1, :] * zc_ext[HALO - sh:HALO - sh + tm, :]
    parts = [(gb * y).astype(BF16)]

    pos = (j * tm + 1 + lax.broadcasted_iota(jnp.int32, (tm, LANES), 0)).astype(F32)
    for gi, w in enumerate(POOL_WINDOWS):
        c0 = gi * LANES
        pg = p[:, c0:c0 + LANES]
        s = pg
        for k in range(1, w):
            s = s + p_ext[HALO - k:HALO - k + tm, c0:c0 + LANES]
        pooled = s / jnp.minimum(pos, float(w)) - pg
        mixed = _dot(pooled.astype(BF16), pw_ref[gi]) * ps_ref[:, c0:c0 + LANES]
        parts.append(mixed.astype(BF16))

    mix = jnp.concatenate(parts, axis=1)
    o_ref[...] = h + _dot(mix, wout_ref[...])

    zc_ext[0:HALO, :] = zc_ext[tm:tm + HALO, :]
    p_ext[0:HALO, :] = p_ext[tm:tm + HALO, :]


def _even_mixer(h_parts, g, w_in, conv_w, pool_w, pool_scale, w_out, *, layer, bsz, seq):
    n, d = h_parts[0].shape
    cdim = w_in.shape[2] // 4
    tm = min(TOKEN_TILE, seq)
    nj = seq // tm
    kern = functools.partial(_even_mixer_kernel, tm=tm, cdim=cdim, nres=len(h_parts))
    return pl.pallas_call(
        kern,
        out_shape=jax.ShapeDtypeStruct((n, d), F32),
        grid=(bsz, nj),
        in_specs=_residual_specs(h_parts, tm, lambda b, j: (b * nj + j, 0)) + [
            _layer_spec(g, layer),
            _layer_spec(w_in, layer),
            _layer_spec(conv_w, layer),
            _layer_spec(pool_w, layer),
            _layer_spec(pool_scale, layer),
            _layer_spec(w_out, layer),
        ],
        out_specs=pl.BlockSpec((tm, d), lambda b, j: (b * nj + j, 0)),
        scratch_shapes=[pltpu.VMEM((HALO + tm, cdim), F32), pltpu.VMEM((HALO + tm, cdim), F32)],
        compiler_params=_params("arbitrary", "arbitrary"),
        name="even_mixer",
    )(*h_parts, g, w_in, conv_w, pool_w, pool_scale, w_out)


def _ffn_kernel(h_ref, g_ref, w1_ref, w3_ref, w2_ref, o_ref, *, fdim):
    h = h_ref[...]
    hn = _rms(h, g_ref[...]).astype(BF16)
    acc = h
    for f0 in range(0, fdim, FFN_F_TILE):
        f1 = min(f0 + FFN_F_TILE, fdim)
        a = _dot(hn, w1_ref[:, f0:f1])
        b = _dot(hn, w3_ref[:, f0:f1])
        act = (a * jax.nn.sigmoid(a) * b).astype(BF16)
        acc = acc + _dot(act, w2_ref[f0:f1, :])
    o_ref[...] = acc


def _ffn(h, g, w1, w3, w2, *, layer):
    n, d = h.shape
    fdim = w1.shape[2]
    tm = min(TOKEN_TILE, n)
    return pl.pallas_call(
        functools.partial(_ffn_kernel, fdim=fdim),
        out_shape=jax.ShapeDtypeStruct((n, d), F32),
        grid=(n // tm,),
        in_specs=[
            pl.BlockSpec((tm, d), lambda i: (i, 0)),
            _layer_spec(g, layer),
            _layer_spec(w1, layer),
            _layer_spec(w3, layer),
            _layer_spec(w2, layer),
        ],
        out_specs=pl.BlockSpec((tm, d), lambda i: (i, 0)),
        compiler_params=_params("arbitrary"),
        name="dense_swiglu",
    )(h, g, w1, w3, w2)


def _odd_in_kernel(h_ref, g_ref, wt_ref, o_ref, *, nck):
    hn = _rms(h_ref[...], g_ref[...]).astype(BF16)
    zt = lax.dot_general(wt_ref[...], hn, (((1,), (1,)), ((), ())), preferred_element_type=F32)
    for c in range(nck):
        o_ref[c] = zt[:, c * LANES:(c + 1) * LANES]


def _odd_in(h, g, w_in_t, *, layer):
    n, d = h.shape
    cols = w_in_t.shape[1]
    tm = min(TOKEN_TILE, n)
    nck = tm // LANES
    return pl.pallas_call(
        functools.partial(_odd_in_kernel, nck=nck),
        out_shape=jax.ShapeDtypeStruct((n // LANES, cols, LANES), F32),
        grid=(n // tm,),
        in_specs=[
            pl.BlockSpec((tm, d), lambda i: (i, 0)),
            _layer_spec(g, layer),
            _layer_spec(w_in_t, layer),
        ],
        out_specs=pl.BlockSpec((nck, cols, LANES), lambda i: (i, 0, 0)),
        compiler_params=_params("arbitrary"),
        name="odd_in_proj",
    )(h, g, w_in_t)


def _s5_kernel(u_ref, kt_ref, ws_ref, wc_ref, coef_ref, y_ref, m_scr, *, hdim, cpb, n_steps):
    t = T_SSM
    row = lax.broadcasted_iota(jnp.int32, (t, t), 0)
    col = lax.broadcasted_iota(jnp.int32, (t, t), 1)
    causal = col >= row
    def expand_row_block(j):
        r0 = j * t if isinstance(j, int) else pl.multiple_of(j * t, t)
        for i in range(hdim):
            kb = jnp.broadcast_to(kt_ref[0, j, i:i + 1, :], (t, t))
            tz = pltpu.roll(kb, 0, 1, stride=1, stride_axis=0)
            m_scr[pl.ds(r0, t), i * t:(i + 1) * t] = jnp.where(causal, tz, 0.0).astype(BF16)

    _static_loop(hdim, expand_row_block)

    x = jnp.concatenate([u_ref[:, c, :] for c in range(hdim)], axis=1).astype(BF16)
    y = _dot(x, m_scr[...])
    st = _dot(x, ws_ref[0])
    half = st.shape[1] // 2
    rb = lax.broadcasted_iota(jnp.int32, st.shape, 0) & (cpb - 1)
    for k in range(n_steps):
        d = 1 << k
        sh = jnp.where(rb >= d, pltpu.roll(st, d, 0), 0.0)
        sw = pltpu.roll(sh, half, 1)
        st = st + coef_ref[0, k, 0:1, :] * sh + coef_ref[0, k, 1:2, :] * sw
    prev = jnp.where(rb >= 1, pltpu.roll(st, 1, 0), 0.0)
    wc = jnp.concatenate([wc_ref[0, i] for i in range(hdim)], axis=1)
    y = y + _dot(prev.astype(BF16), wc)
    for i in range(hdim):
        y_ref[:, i, :] = y[:, i * t:(i + 1) * t]


def _s5(zt3, kt, ws, wc, coef, *, layer, groups, cpb):
    nc = zt3.shape[0]
    g, hdim = groups, kt.shape[1]
    n_steps = coef.shape[1]
    g0 = layer * groups
    return pl.pallas_call(
        functools.partial(_s5_kernel, hdim=hdim, cpb=cpb, n_steps=n_steps),
        out_shape=jax.ShapeDtypeStruct((nc, g * hdim, T_SSM), F32),
        grid=(g,),
        in_specs=[
            pl.BlockSpec((nc, hdim, T_SSM), lambda i: (0, i, 0)),
            pl.BlockSpec((1,) + kt.shape[1:], lambda i: (g0 + i, 0, 0, 0)),
            pl.BlockSpec((1,) + ws.shape[1:], lambda i: (g0 + i, 0, 0)),
            pl.BlockSpec((1,) + wc.shape[1:], lambda i: (g0 + i, 0, 0, 0)),
            pl.BlockSpec((1,) + coef.shape[1:], lambda i: (g0 + i, 0, 0, 0)),
        ],
        out_specs=pl.BlockSpec((nc, hdim, T_SSM), lambda i: (0, i, 0)),
        scratch_shapes=[pltpu.VMEM((hdim * T_SSM, hdim * T_SSM), BF16)],
        compiler_params=_params("arbitrary"),
        name="s5_chunked",
    )(zt3, kt, ws, wc, coef)


def _s5_tables(lam_re, lam_im, log_dt, b_re, b_im, c_re, c_im, *, cpb):
    flat = lambda v: v.reshape((-1,) + v.shape[2:])
    cat = lambda u, v: jnp.concatenate([u, v], axis=-1)
    lr, li, log_dt = flat(lam_re), flat(lam_im), flat(log_dt)
    b_re, b_im, c_re, c_im = flat(b_re), flat(b_im), flat(c_re), flat(c_im)
    gg, p = lr.shape
    hdim = b_re.shape[2]
    t = T_SSM
    dt = jnp.exp(log_dt)[:, None]
    mag = jnp.exp(lr * dt)
    abar_r = mag * jnp.cos(li * dt)
    abar_i = mag * jnp.sin(li * dt)
    qr, qi = abar_r - 1.0, abar_i
    den = lr * lr + li * li
    fr = ((qr * lr + qi * li) / den)[..., None]
    fi = ((qi * lr - qr * li) / den)[..., None]
    bjr = (fr * b_re - fi * b_im).transpose(0, 2, 1)
    bji = (fr * b_im + fi * b_re).transpose(0, 2, 1)

    def apow(k):
        kk = k[None, :, None]
        m = jnp.exp(kk * (lr * dt)[:, None, :])
        th = kk * (li * dt)[:, None, :]
        return m * jnp.cos(th), m * jnp.sin(th)

    par, pai = apow(jnp.arange(t + 1, dtype=F32))
    pr, pi = par[:, :t], pai[:, :t]
    p1r, p1i = par[:, 1:], pai[:, 1:]
    cbr = c_re[:, None] * bjr[:, :, None, :] - c_im[:, None] * bji[:, :, None, :]
    cbi = c_re[:, None] * bji[:, :, None, :] + c_im[:, None] * bjr[:, :, None, :]
    kt = jnp.einsum('gjic,glc->gjil', cat(cbr, -cbi), cat(pr, pi),
                    precision=lax.Precision.HIGHEST)
    er, ei = pr[:, ::-1], pi[:, ::-1]
    ws = (cat(er, er)[:, None] * cat(bjr, bji)[:, :, None, :]
          + cat(-ei, ei)[:, None] * cat(bji, bjr)[:, :, None, :])
    ws = ws.astype(BF16).reshape(gg, hdim * t, 2 * p)
    p1rt = p1r.transpose(0, 2, 1)
    p1it = p1i.transpose(0, 2, 1)
    wc = (cat(c_re, -c_im)[..., None] * jnp.concatenate([p1rt, p1rt], axis=1)[:, None]
          + cat(-c_im, -c_re)[..., None] * jnp.concatenate([p1it, p1it], axis=1)[:, None])
    wc = wc.astype(BF16)
    n_steps = max(1, int(math.log2(cpb)))
    sr, si = apow(float(t) * (2.0 ** jnp.arange(n_steps, dtype=F32)))
    coef = jnp.stack([cat(sr, sr), cat(-si, si)], axis=2)
    return kt, ws, wc, coef


def _odd_tail_kernel(z_ref, y_ref, h_ref, d_ref, gw_ref, gb_ref, lng_ref, sw_ref, sb_ref,
                     wout_ref, g2_ref, rwh_ref, rwl_ref, o_ref, xn_ref, route_ref,
                     *, nck, sdim, heads, n_experts):
    def cat(ref, r0, r1):
        return jnp.concatenate([ref[c, r0:r1, :] for c in range(nck)], axis=1)

    def rep(ref, r0=None, r1=None):
        v = ref[...] if r0 is None else ref[r0:r1, :]
        return jnp.concatenate([v] * nck, axis=1)

    ut = cat(z_ref, 0, sdim)
    yt = cat(y_ref, 0, sdim)
    gt = jax.nn.gelu(yt + rep(d_ref) * ut)
    glu = _dot(gw_ref[...], gt.astype(BF16)) + rep(gb_ref)
    parts = [gt * jax.nn.sigmoid(glu)]

    hd = sdim // heads
    for hh in range(heads):
        r0 = sdim + hh * hd
        up = jax.nn.gelu(cat(z_ref, r0, r0 + hd))
        vp = jax.nn.gelu(cat(z_ref, sdim + r0, sdim + r0 + hd))
        mu = jnp.mean(vp, axis=0, keepdims=True)
        vc = vp - mu
        var = jnp.mean(vc * vc, axis=0, keepdims=True)
        vn = vc * lax.rsqrt(var + LN_EPS) * rep(lng_ref, hh * hd, (hh + 1) * hd)
        stacked = jnp.concatenate([vn[:, c * LANES:(c + 1) * LANES] for c in range(nck)], axis=0)
        s = _dot(stacked.astype(BF16), sw_ref[hh]) + sb_ref[hh]
        s = jnp.concatenate([s[c * hd:(c + 1) * hd, :] for c in range(nck)], axis=1)
        parts.append(up * s)

    mix = jnp.concatenate(parts, axis=0).T.astype(BF16)
    h1 = h_ref[...] + _dot(mix, wout_ref[...])
    o_ref[...] = h1

    xn = _rms(h1, g2_ref[...])
    xn_ref[...] = xn
    xh = xn.astype(BF16)
    xl = (xn - xh.astype(F32)).astype(BF16)
    logits = _dot(xh, rwh_ref[...]) + (_dot(xh, rwl_ref[...]) + _dot(xl, rwh_ref[...]))
    lane = lax.broadcasted_iota(jnp.int32, logits.shape, 1)
    neg = jnp.float32(-jnp.inf)
    lg = jnp.where(lane < n_experts, logits, neg)
    m1 = jnp.max(lg, axis=1, keepdims=True)
    i1 = jnp.min(jnp.where(lg == m1, lane, LANES), axis=1, keepdims=True)
    lg2 = jnp.where(lane == i1, neg, lg)
    m2 = jnp.max(lg2, axis=1, keepdims=True)
    i2 = jnp.min(jnp.where(lg2 == m2, lane, LANES), axis=1, keepdims=True)
    e2 = jnp.exp(m2 - m1)
    w1 = 1.0 / (1.0 + e2)
    w2 = e2 / (1.0 + e2)
    route_ref[...] = (jnp.where(lane == ID_LANE, i1.astype(F32), 0.0)
                      + jnp.where(lane == ID_LANE + 1, i2.astype(F32), 0.0)
                      + jnp.where(lane == GATE_LANE, w1, 0.0)
                      + jnp.where(lane == GATE_LANE + 1, w2, 0.0))


def _odd_tail(zt3, yt3, h, d_b, glu_wt, glu_b_b, lng_b, sgu_wt, sgu_b, w_out, g2, router_hi,
              router_lo,
              *, layer, heads, n_experts):
    n, dm = h.shape
    cols = zt3.shape[1]
    sdim = yt3.shape[1]
    tm = min(TOKEN_TILE, n)
    nck = tm // LANES
    kern = functools.partial(_odd_tail_kernel, nck=nck, sdim=sdim, heads=heads, n_experts=n_experts)
    return pl.pallas_call(
        kern,
        out_shape=(jax.ShapeDtypeStruct((n, dm), F32),
                   jax.ShapeDtypeStruct((n, dm), F32),
                   jax.ShapeDtypeStruct((n, LANES), F32)),
        grid=(n // tm,),
        in_specs=[
            pl.BlockSpec((nck, cols, LANES), lambda i: (i, 0, 0)),
            pl.BlockSpec((nck, sdim, LANES), lambda i: (i, 0, 0)),
            pl.BlockSpec((tm, dm), lambda i: (i, 0)),
            _const_spec(d_b.shape),
            _layer_spec(glu_wt, layer),
            _const_spec(glu_b_b.shape),
            _const_spec(lng_b.shape),
            _const_spec(sgu_wt.shape),
            _const_spec(sgu_b.shape),
            _layer_spec(w_out, layer),
            _layer_spec(g2, layer),
            _const_spec(router_hi.shape),
            _const_spec(router_lo.shape),
        ],
        out_specs=(pl.BlockSpec((tm, dm), lambda i: (i, 0)),
                   pl.BlockSpec((tm, dm), lambda i: (i, 0)),
                   pl.BlockSpec((tm, LANES), lambda i: (i, 0))),
        compiler_params=_params("arbitrary"),
        name="odd_mixer_tail",
    )(zt3, yt3, h, d_b, glu_wt, glu_b_b, lng_b, sgu_wt, sgu_b, w_out, g2, router_hi, router_lo)


def _experts_kernel(te_ref, ns_ref, na_ref, x_ref, w1_ref, w3_ref, w2_ref, o_ref, *, sub):
    f = pl.program_id(1)

    @pl.when(f == 0)
    def _():
        o_ref[...] = jnp.zeros(o_ref.shape, F32)

    def sub_tile(s, carry):
        rows = pl.ds(pl.multiple_of(s * sub, sub), sub)
        x = x_ref[rows, :].astype(BF16)
        a = _dot(x, w1_ref[0].astype(BF16))
        b = _dot(x, w3_ref[0].astype(BF16))
        act = (a * jax.nn.sigmoid(a) * b).astype(BF16)
        o_ref[rows, :] += _dot(act, w2_ref[0].astype(BF16))
        return carry

    lax.fori_loop(0, ns_ref[pl.program_id(0)], sub_tile, 0)


def _experts(tile_expert, tile_nsub, n_active, xs, w1, w3, w2, *, layer):
    r, d = xs.shape
    fdim = w1.shape[3]
    te = EXPERT_TILE
    tf = min(EXPERT_F_TILE, fdim)
    n_tiles = r // te
    nf = fdim // tf

    def row_map(i, f, te_ref, ns_ref, na_ref):
        return (jnp.minimum(i, na_ref[0] - 1), 0)

    grid_spec = pltpu.PrefetchScalarGridSpec(
        num_scalar_prefetch=3,
        grid=(n_tiles, nf),
        in_specs=[
            pl.BlockSpec((te, d), row_map),
            pl.BlockSpec((None, 1, d, tf), lambda i, f, te_ref, *_: (layer, te_ref[i], 0, f)),
            pl.BlockSpec((None, 1, d, tf), lambda i, f, te_ref, *_: (layer, te_ref[i], 0, f)),
            pl.BlockSpec((None, 1, tf, d), lambda i, f, te_ref, *_: (layer, te_ref[i], f, 0)),
        ],
        out_specs=pl.BlockSpec((te, d), lambda i, f, *_: (i, 0)),
    )
    return pl.pallas_call(
        functools.partial(_experts_kernel, sub=min(EXPERT_SUB, te)),
        out_shape=jax.ShapeDtypeStruct((r, d), F32),
        grid_spec=grid_spec,
        compiler_params=_params("arbitrary", "arbitrary"),
        name="grouped_experts",
    )(tile_expert, tile_nsub, n_active, xs, w1, w3, w2)


def _dispatch_kernel(te_ref, off_ref, cnt_ref, cb_ref, ct_ref, o_ref, *, te, nb, n):
    i = pl.program_id(0)
    e = te_ref[i]
    row = i * te + lax.broadcasted_iota(jnp.int32, (1, te), 1)
    rank = row - off_ref[e]
    blk_end = cb_ref[0][:, LANES - 1:LANES]
    blk = jnp.sum((blk_end <= rank).astype(jnp.int32), axis=0, keepdims=True)
    blk = jnp.minimum(blk, nb - 1)
    onehot = (lax.broadcasted_iota(jnp.int32, (nb, te), 0) == blk).astype(F32)
    counts_t = jnp.dot(ct_ref[0].astype(F32), onehot, precision=lax.Precision.HIGHEST,
                       preferred_element_type=F32)
    within = jnp.sum((counts_t <= rank.astype(F32)).astype(jnp.int32), axis=0, keepdims=True)
    filler = row & (n - 1) if n & (n - 1) == 0 else lax.rem(row, n)
    o_ref[0] = jnp.where(rank < cnt_ref[e], blk * LANES + within, filler)


def _dispatch_rows(tile_expert, offs, counts, cb, ct, *, te, n):
    n_exp, nb, _ = cb.shape
    n_tiles = tile_expert.shape[0]
    grid_spec = pltpu.PrefetchScalarGridSpec(
        num_scalar_prefetch=3,
        grid=(n_tiles,),
        in_specs=[
            pl.BlockSpec((1, nb, LANES), lambda i, te_ref, *_: (te_ref[i], 0, 0)),
            pl.BlockSpec((1, LANES, nb), lambda i, te_ref, *_: (te_ref[i], 0, 0)),
        ],
        out_specs=pl.BlockSpec((1, 1, te), lambda i, *_: (i, 0, 0)),
    )
    return pl.pallas_call(
        functools.partial(_dispatch_kernel, te=te, nb=nb, n=n),
        out_shape=jax.ShapeDtypeStruct((n_tiles, 1, te), jnp.int32),
        grid_spec=grid_spec,
        compiler_params=_params("arbitrary"),
        name="dispatch_rows",
    )(tile_expert, offs, counts, cb, ct)


def _moe(h1, xn, route, w1, w3, w2, *, layer):
    n, d = h1.shape
    n_exp = w1.shape[1]
    te = EXPERT_TILE
    sub = min(EXPERT_SUB, te)
    n_tiles = (n * TOP_K + te - 1) // te + n_exp
    nb = n // LANES

    ids = route[:, ID_LANE:ID_LANE + TOP_K].astype(jnp.int32)
    chosen = ids[:, :, None] == jnp.arange(n_exp, dtype=jnp.int32)[None, None, :]
    sel = chosen.any(axis=1)
    csum = jnp.cumsum(sel.astype(jnp.int32), axis=0)
    counts = csum[-1]
    padded = ((counts + te - 1) // te) * te
    ends = jnp.cumsum(padded)
    offs = ends - padded
    tile_start = jnp.arange(n_tiles, dtype=jnp.int32) * te
    in_use = tile_start < ends[-1]
    n_active = (ends[-1] // te).astype(jnp.int32)
    tile_expert = jnp.sum(tile_start[:, None] >= ends[None, :], axis=1).astype(jnp.int32)
    last_expert = jnp.take(tile_expert, jnp.maximum(n_active - 1, 0))
    tile_expert = jnp.where(in_use, tile_expert, last_expert)
    valid = jnp.clip(counts[tile_expert] - (tile_start - offs[tile_expert]), 0, te)
    tile_nsub = jnp.where(in_use, (valid + sub - 1) // sub, 0).astype(jnp.int32)

    cb = csum.T.reshape(n_exp, nb, LANES)
    src = _dispatch_rows(tile_expert, offs.astype(jnp.int32), counts.astype(jnp.int32), cb,
                         cb.transpose(0, 2, 1), te=te, n=n)
    xs = jnp.take(xn, src.reshape(-1), axis=0, mode='clip')
    ys = _experts(tile_expert, tile_nsub, n_active.reshape(1), xs, w1, w3, w2, layer=layer)
    dest = offs[None, :] + csum - 1
    pos = jnp.sum(jnp.where(chosen, dest[:, None, :], 0), axis=2)
    return (h1, jnp.take(ys, pos[:, 0], axis=0, mode='clip'),
            jnp.take(ys, pos[:, 1], axis=0, mode='clip'), route)


def _final_norm_kernel(*refs):
    *res_refs, g_ref, o_ref = refs
    o_ref[...] = _rms(_residual(res_refs), g_ref[...])


def _final_norm(h_parts, g):
    n, d = h_parts[0].shape
    tm = min(TOKEN_TILE, n)
    return pl.pallas_call(
        _final_norm_kernel,
        out_shape=jax.ShapeDtypeStruct((n, d), F32),
        grid=(n // tm,),
        in_specs=_residual_specs(h_parts, tm, lambda i: (i, 0)) + [_const_spec((1, d))],
        out_specs=pl.BlockSpec((tm, d), lambda i: (i, 0)),
        compiler_params=_params("arbitrary"),
        name="final_norm",
    )(*h_parts, g)


def _even_layer(h_parts, p, i, *, bsz, seq):
    h = _even_mixer(h_parts, p['ev_norm1_g'], p['ev_w_in'], p['ev_conv_w'], p['ev_pool_w'],
                    p['ev_pool_scale'], p['ev_w_out'], layer=i, bsz=bsz, seq=seq)
    return _ffn(h, p['ev_norm2_g'], p['ev_ffn_w1'], p['ev_ffn_w3'], p['ev_ffn_w2'], layer=i)


def _odd_layer(h, p, i, *, seq):
    sgu_w = p['od_sgu_w'][i]
    router_w = p['od_router_w'][i]
    heads, slen, _ = sgu_w.shape
    n_exp = router_w.shape[1]
    assert slen == LANES and seq % T_SSM == 0
    cpb = seq // T_SSM
    assert cpb & (cpb - 1) == 0, "chunks per sequence must be a power of two"

    zt3 = _odd_in(h, p['od_norm1_g'], p['od_w_in_t'], layer=i)
    yt3 = _s5(zt3, *p['s5_tables'], layer=i, groups=p['od_lambda_re'].shape[1], cpb=cpb)

    lanes_b = lambda v: jnp.broadcast_to(v[:, None], (v.shape[0], LANES))
    cidx = jnp.arange(slen) // CHUNK
    mask = cidx[None, :] <= cidx[:, None]
    sgu_wt = jnp.where(mask[None], sgu_w, 0.0).transpose(0, 2, 1).astype(BF16)
    router_pad = jnp.zeros((router_w.shape[0], LANES), F32).at[:, :n_exp].set(router_w)
    router_hi = router_pad.astype(BF16)
    router_lo = (router_pad - router_hi.astype(F32)).astype(BF16)
    h1, xn, route = _odd_tail(
        zt3, yt3, h, lanes_b(p['od_d'][i]), p['od_glu_wt'], lanes_b(p['od_glu_b'][i]),
        lanes_b(p['od_sgu_ln_g'][i]), sgu_wt, p['od_sgu_b'][i][:, None, :], p['od_w_out'],
        p['od_norm2_g'], router_hi, router_lo, layer=i, heads=heads, n_experts=n_exp)
    return _moe(h1, xn, route, p['od_moe_w1'], p['od_moe_w3'], p['od_moe_w2'], layer=i)


def kernel(x, ev_norm1_g, ev_w_in, ev_conv_w, ev_pool_w, ev_pool_scale, ev_w_out, ev_norm2_g, ev_ffn_w1, ev_ffn_w3, ev_ffn_w2, od_norm1_g, od_w_in, od_lambda_re, od_lambda_im, od_log_dt, od_b_re, od_b_im, od_c_re, od_c_im, od_d, od_glu_w, od_glu_b, od_sgu_ln_g, od_sgu_w, od_sgu_b, od_w_out, od_norm2_g, od_router_w, od_moe_w1, od_moe_w3, od_moe_w2, final_norm_g):
    bsz, seq, d = x.shape
    depth = ev_norm1_g.shape[0] + od_norm1_g.shape[0]
    row = lambda v: v[:, None, :]
    p = dict(
        ev_norm1_g=row(ev_norm1_g), ev_w_in=ev_w_in.astype(BF16), ev_conv_w=ev_conv_w,
        ev_pool_w=ev_pool_w.astype(BF16), ev_pool_scale=row(ev_pool_scale),
        ev_w_out=ev_w_out.astype(BF16), ev_norm2_g=row(ev_norm2_g),
        ev_ffn_w1=ev_ffn_w1.astype(BF16), ev_ffn_w3=ev_ffn_w3.astype(BF16),
        ev_ffn_w2=ev_ffn_w2.astype(BF16),
        od_norm1_g=row(od_norm1_g), od_w_in_t=od_w_in.transpose(0, 2, 1).astype(BF16),
        od_lambda_re=od_lambda_re, od_lambda_im=od_lambda_im, od_log_dt=od_log_dt,
        od_b_re=od_b_re, od_b_im=od_b_im, od_c_re=od_c_re, od_c_im=od_c_im, od_d=od_d,
        od_glu_wt=od_glu_w.transpose(0, 2, 1).astype(BF16), od_glu_b=od_glu_b,
        od_sgu_ln_g=od_sgu_ln_g, od_sgu_w=od_sgu_w, od_sgu_b=od_sgu_b,
        od_w_out=od_w_out.astype(BF16), od_norm2_g=row(od_norm2_g), od_router_w=od_router_w,
        od_moe_w1=od_moe_w1, od_moe_w3=od_moe_w3, od_moe_w2=od_moe_w2,
    )
    assert seq % T_SSM == 0
    p['s5_tables'] = _s5_tables(od_lambda_re, od_lambda_im, od_log_dt, od_b_re, od_b_im, od_c_re,
                                od_c_im, cpb=seq // T_SSM)
    h_parts = (x.reshape(bsz * seq, d),)
    for layer in range(depth):
        if layer % 2 == 0:
            h_parts = (_even_layer(h_parts, p, layer // 2, bsz=bsz, seq=seq),)
        else:
            assert len(h_parts) == 1
            h_parts = _odd_layer(h_parts[0], p, layer // 2, seq=seq)
    return _final_norm(h_parts, final_norm_g[None]).reshape(bsz, seq, d)
```

```python
import functools
import math

import jax
import jax.numpy as jnp
from jax import lax
from jax.experimental import pallas as pl
from jax.experimental.pallas import tpu as pltpu

F32 = jnp.float32
BF16 = jnp.bfloat16

RMS_EPS = 1e-5
LN_EPS = 1e-5
CHUNK = 64
POOL_WINDOWS = (2, 4, 8, 16)
CONV_WIDTH = 3
TOP_K = 2

LANES = 128
V7X_VMEM_BYTES = 64 * 1024 * 1024
VMEM_LIMIT = V7X_VMEM_BYTES * 7 // 8
HALO = 16
T_SSM = LANES
TOKEN_TILE = 512
LIGHT_TOKEN_TILE = 1024
EVEN_ROW_BLOCKS = 2
ID_LANE = LANES - 4
GATE_LANE = LANES - 2
EXPERT_TILE = 2048
EXPERT_SUB = 512
EXPERT_F_TILE = 512
FFN_F_TILE = 1024


def _const_spec(shape):
    nd = len(shape)
    return pl.BlockSpec(shape, lambda *_: (0,) * nd, pipeline_mode=pl.Buffered(1))


def _layer_spec(arr, layer):
    nd = arr.ndim - 1
    return pl.BlockSpec((None,) + arr.shape[1:], lambda *_: (layer,) + (0,) * nd,
                        pipeline_mode=pl.Buffered(1))


def _params(*sem):
    return pltpu.CompilerParams(dimension_semantics=sem, vmem_limit_bytes=VMEM_LIMIT)


def _rms(x, g):
    return x * lax.rsqrt(jnp.mean(x * x, axis=-1, keepdims=True) + RMS_EPS) * g


def _dot(a, b):
    return jnp.dot(a, b, preferred_element_type=F32)


def _static_loop(n, body):
    for j in range(n):
        body(j)


def _residual(refs, rows=slice(None)):
    if len(refs) == 1:
        return refs[0][rows, :]
    h_ref, ya_ref, yb_ref, route_ref = refs
    route = route_ref[rows, :]
    ga = route[:, GATE_LANE:GATE_LANE + 1]
    gb = route[:, GATE_LANE + 1:GATE_LANE + 2]
    return h_ref[rows, :] + ga * ya_ref[rows, :] + gb * yb_ref[rows, :]


def _residual_specs(h_parts, tm, index_map):
    return [pl.BlockSpec((tm, p.shape[1]), index_map) for p in h_parts]


def _even_mixer_kernel(*refs, tm, cdim, nres):
    res_refs = refs[:nres]
    g_ref, win_ref, cw_ref, pw_ref, ps_ref, wout_ref, o_ref, zc_ext, p_ext = refs[nres:]
    j = pl.program_id(1)

    @pl.when(j == 0)
    def _():
        zc_ext[0:HALO, :] = jnp.zeros((HALO, cdim), F32)
        p_ext[0:HALO, :] = jnp.zeros((HALO, cdim), F32)

    rb = tm // EVEN_ROW_BLOCKS
    for q in range(EVEN_ROW_BLOCKS):
        r0 = q * rb
        e0 = HALO + r0
        h = _residual(res_refs, slice(r0, r0 + rb))
        hn = _rms(h, g_ref[...]).astype(BF16)
        z = _dot(hn, win_ref[...])
        gb = z[:, 0:cdim]
        zc = z[:, cdim:2 * cdim] * z[:, 2 * cdim:3 * cdim]
        p = z[:, 3 * cdim:4 * cdim]
        zc_ext[e0:e0 + rb, :] = zc
        p_ext[e0:e0 + rb, :] = p

        y = cw_ref[CONV_WIDTH - 1:CONV_WIDTH, :] * zc
        for k in range(CONV_WIDTH - 1):
            sh = CONV_WIDTH - 1 - k
            y = y + cw_ref[k:k + 1, :] * zc_ext[e0 - sh:e0 - sh + rb, :]
        parts = [(gb * y).astype(BF16)]

        pos = (j * tm + r0 + 1 + lax.broadcasted_iota(jnp.int32, (rb, LANES), 0)).astype(F32)
        for gi, w in enumerate(POOL_WINDOWS):
            c0 = gi * LANES
            pg = p[:, c0:c0 + LANES]
            s = pg
            for k in range(1, w):
                s = s + p_ext[e0 - k:e0 - k + rb, c0:c0 + LANES]
            pooled = s / jnp.minimum(pos, float(w)) - pg
            mixed = _dot(pooled.astype(BF16), pw_ref[gi]) * ps_ref[:, c0:c0 + LANES]
            parts.append(mixed.astype(BF16))

        mix = jnp.concatenate(parts, axis=1)
        o_ref[r0:r0 + rb, :] = h + _dot(mix, wout_ref[...])

    zc_ext[0:HALO, :] = zc_ext[tm:tm + HALO, :]
    p_ext[0:HALO, :] = p_ext[tm:tm + HALO, :]


def _even_mixer(h_parts, g, w_in, conv_w, pool_w, pool_scale, w_out, *, layer, bsz, seq):
    n, d = h_parts[0].shape
    cdim = w_in.shape[2] // 4
    tm = min(TOKEN_TILE, seq)
    nj = seq // tm
    kern = functools.partial(_even_mixer_kernel, tm=tm, cdim=cdim, nres=len(h_parts))
    return pl.pallas_call(
        kern,
        out_shape=jax.ShapeDtypeStruct((n, d), F32),
        grid=(bsz, nj),
        in_specs=_residual_specs(h_parts, tm, lambda b, j: (b * nj + j, 0)) + [
            _layer_spec(g, layer),
            _layer_spec(w_in, layer),
            _layer_spec(conv_w, layer),
            _layer_spec(pool_w, layer),
            _layer_spec(pool_scale, layer),
            _layer_spec(w_out, layer),
        ],
        out_specs=pl.BlockSpec((tm, d), lambda b, j: (b * nj + j, 0)),
        scratch_shapes=[pltpu.VMEM((HALO + tm, cdim), F32), pltpu.VMEM((HALO + tm, cdim), F32)],
        compiler_params=_params("arbitrary", "arbitrary"),
        name="even_mixer",
    )(*h_parts, g, w_in, conv_w, pool_w, pool_scale, w_out)


def _ffn_kernel(h_ref, g_ref, w1_ref, w3_ref, w2_ref, o_ref, *, fdim):
    h = h_ref[...]
    hn = _rms(h, g_ref[...]).astype(BF16)
    acc = h
    for f0 in range(0, fdim, FFN_F_TILE):
        f1 = min(f0 + FFN_F_TILE, fdim)
        a = _dot(hn, w1_ref[:, f0:f1])
        b = _dot(hn, w3_ref[:, f0:f1])
        act = (a * jax.nn.sigmoid(a) * b).astype(BF16)
        acc = acc + _dot(act, w2_ref[f0:f1, :])
    o_ref[...] = acc


def _ffn(h, g, w1, w3, w2, *, layer):
    n, d = h.shape
    fdim = w1.shape[2]
    tm = min(TOKEN_TILE, n)
    return pl.pallas_call(
        functools.partial(_ffn_kernel, fdim=fdim),
        out_shape=jax.ShapeDtypeStruct((n, d), F32),
        grid=(n // tm,),
        in_specs=[
            pl.BlockSpec((tm, d), lambda i: (i, 0)),
            _layer_spec(g, layer),
            _layer_spec(w1, layer),
            _layer_spec(w3, layer),
            _layer_spec(w2, layer),
        ],
        out_specs=pl.BlockSpec((tm, d), lambda i: (i, 0)),
        compiler_params=_params("arbitrary"),
        name="dense_swiglu",
    )(h, g, w1, w3, w2)


def _odd_in_kernel(h_ref, g_ref, wt_ref, o_ref, *, nck):
    hn = _rms(h_ref[...], g_ref[...]).astype(BF16)
    zt = lax.dot_general(wt_ref[...], hn, (((1,), (1,)), ((), ())), preferred_element_type=F32)
    for c in range(nck):
        o_ref[c] = zt[:, c * LANES:(c + 1) * LANES]


def _odd_in(h, g, w_in_t, *, layer):
    n, d = h.shape
    cols = w_in_t.shape[1]
    tm = min(LIGHT_TOKEN_TILE, n)
    nck = tm // LANES
    return pl.pallas_call(
        functools.partial(_odd_in_kernel, nck=nck),
        out_shape=jax.ShapeDtypeStruct((n // LANES, cols, LANES), F32),
        grid=(n // tm,),
        in_specs=[
            pl.BlockSpec((tm, d), lambda i: (i, 0)),
            _layer_spec(g, layer),
            _layer_spec(w_in_t, layer),
        ],
        out_specs=pl.BlockSpec((nck, cols, LANES), lambda i: (i, 0, 0)),
        compiler_params=_params("arbitrary"),
        name="odd_in_proj",
    )(h, g, w_in_t)


def _s5_kernel(u_ref, kt_ref, ws_ref, wc_ref, coef_ref, y_ref, m_scr, *, hdim, cpb, n_steps):
    t = T_SSM
    row = lax.broadcasted_iota(jnp.int32, (t, t), 0)
    col = lax.broadcasted_iota(jnp.int32, (t, t), 1)
    causal = col >= row
    def expand_row_block(j):
        r0 = j * t if isinstance(j, int) else pl.multiple_of(j * t, t)
        for i in range(hdim):
            kb = jnp.broadcast_to(kt_ref[0, j, i:i + 1, :], (t, t))
            tz = pltpu.roll(kb, 0, 1, stride=1, stride_axis=0)
            m_scr[pl.ds(r0, t), i * t:(i + 1) * t] = jnp.where(causal, tz, 0.0).astype(BF16)

    _static_loop(hdim, expand_row_block)

    x = jnp.concatenate([u_ref[:, c, :] for c in range(hdim)], axis=1).astype(BF16)
    y = _dot(x, m_scr[...])
    st = _dot(x, ws_ref[0])
    half = st.shape[1] // 2
    rb = lax.broadcasted_iota(jnp.int32, st.shape, 0) & (cpb - 1)
    for k in range(n_steps):
        d = 1 << k
        sh = jnp.where(rb >= d, pltpu.roll(st, d, 0), 0.0)
        sw = pltpu.roll(sh, half, 1)
        st = st + coef_ref[0, k, 0:1, :] * sh + coef_ref[0, k, 1:2, :] * sw
    prev = jnp.where(rb >= 1, pltpu.roll(st, 1, 0), 0.0)
    wc = jnp.concatenate([wc_ref[0, i] for i in range(hdim)], axis=1)
    y = y + _dot(prev.astype(BF16), wc)
    for i in range(hdim):
        y_ref[:, i, :] = y[:, i * t:(i + 1) * t]


def _s5(zt3, kt, ws, wc, coef, *, layer, groups, cpb):
    nc = zt3.shape[0]
    g, hdim = groups, kt.shape[1]
    n_steps = coef.shape[1]
    g0 = layer * groups
    return pl.pallas_call(
        functools.partial(_s5_kernel, hdim=hdim, cpb=cpb, n_steps=n_steps),
        out_shape=jax.ShapeDtypeStruct((nc, g * hdim, T_SSM), F32),
        grid=(g,),
        in_specs=[
            pl.BlockSpec((nc, hdim, T_SSM), lambda i: (0, i, 0)),
            pl.BlockSpec((1,) + kt.shape[1:], lambda i: (g0 + i, 0, 0, 0)),
            pl.BlockSpec((1,) + ws.shape[1:], lambda i: (g0 + i, 0, 0)),
            pl.BlockSpec((1,) + wc.shape[1:], lambda i: (g0 + i, 0, 0, 0)),
            pl.BlockSpec((1,) + coef.shape[1:], lambda i: (g0 + i, 0, 0, 0)),
        ],
        out_specs=pl.BlockSpec((nc, hdim, T_SSM), lambda i: (0, i, 0)),
        scratch_shapes=[pltpu.VMEM((hdim * T_SSM, hdim * T_SSM), BF16)],
        compiler_params=_params("arbitrary"),
        name="s5_chunked",
    )(zt3, kt, ws, wc, coef)


def _s5_tables(lam_re, lam_im, log_dt, b_re, b_im, c_re, c_im, *, cpb):
    flat = lambda v: v.reshape((-1,) + v.shape[2:])
    cat = lambda u, v: jnp.concatenate([u, v], axis=-1)
    lr, li, log_dt = flat(lam_re), flat(lam_im), flat(log_dt)
    b_re, b_im, c_re, c_im = flat(b_re), flat(b_im), flat(c_re), flat(c_im)
    gg, p = lr.shape
    hdim = b_re.shape[2]
    t = T_SSM
    dt = jnp.exp(log_dt)[:, None]
    mag = jnp.exp(lr * dt)
    abar_r = mag * jnp.cos(li * dt)
    abar_i = mag * jnp.sin(li * dt)
    qr, qi = abar_r - 1.0, abar_i
    den = lr * lr + li * li
    fr = ((qr * lr + qi * li) / den)[..., None]
    fi = ((qi * lr - qr * li) / den)[..., None]
    bjr = (fr * b_re - fi * b_im).transpose(0, 2, 1)
    bji = (fr * b_im + fi * b_re).transpose(0, 2, 1)

    def apow(k):
        kk = k[None, :, None]
        m = jnp.exp(kk * (lr * dt)[:, None, :])
        th = kk * (li * dt)[:, None, :]
        return m * jnp.cos(th), m * jnp.sin(th)

    par, pai = apow(jnp.arange(t + 1, dtype=F32))
    pr, pi = par[:, :t], pai[:, :t]
    p1r, p1i = par[:, 1:], pai[:, 1:]
    cbr = c_re[:, None] * bjr[:, :, None, :] - c_im[:, None] * bji[:, :, None, :]
    cbi = c_re[:, None] * bji[:, :, None, :] + c_im[:, None] * bjr[:, :, None, :]
    kt = jnp.einsum('gjic,glc->gjil', cat(cbr, -cbi), cat(pr, pi),
                    precision=lax.Precision.HIGHEST)
    er, ei = apow(jnp.arange(t - 1, -1, -1, dtype=F32))
    ws = (cat(er, er)[:, None] * cat(bjr, bji)[:, :, None, :]
          + cat(-ei, ei)[:, None] * cat(bji, bjr)[:, :, None, :])
    ws = ws.astype(BF16).reshape(gg, hdim * t, 2 * p)
    p1rt = p1r.transpose(0, 2, 1)
    p1it = p1i.transpose(0, 2, 1)
    wc = (cat(c_re, -c_im)[..., None] * jnp.concatenate([p1rt, p1rt], axis=1)[:, None]
          + cat(-c_im, -c_re)[..., None] * jnp.concatenate([p1it, p1it], axis=1)[:, None])
    wc = wc.astype(BF16)
    n_steps = max(1, int(math.log2(cpb)))
    sr, si = apow(float(t) * (2.0 ** jnp.arange(n_steps, dtype=F32)))
    coef = jnp.stack([cat(sr, sr), cat(-si, si)], axis=2)
    return kt, ws, wc, coef


def _odd_tail_kernel(z_ref, y_ref, h_ref, d_ref, gw_ref, gb_ref, lng_ref, sw_ref, sb_ref,
                     wout_ref, g2_ref, rwh_ref, rwl_ref, o_ref, xn_ref, route_ref,
                     *, nck, sdim, heads, n_experts):
    def cat(ref, r0, r1):
        return jnp.concatenate([ref[c, r0:r1, :] for c in range(nck)], axis=1)

    def rep(ref, r0=None, r1=None):
        v = ref[...] if r0 is None else ref[r0:r1, :]
        return jnp.concatenate([v] * nck, axis=1)

    ut = cat(z_ref, 0, sdim)
    yt = cat(y_ref, 0, sdim)
    gt = jax.nn.gelu(yt + rep(d_ref) * ut)
    glu = _dot(gw_ref[...], gt.astype(BF16)) + rep(gb_ref)
    parts = [gt * jax.nn.sigmoid(glu)]

    hd = sdim // heads
    for hh in range(heads):
        r0 = sdim + hh * hd
        up = jax.nn.gelu(cat(z_ref, r0, r0 + hd))
        vp = jax.nn.gelu(cat(z_ref, sdim + r0, sdim + r0 + hd))
        mu = jnp.mean(vp, axis=0, keepdims=True)
        vc = vp - mu
        var = jnp.mean(vc * vc, axis=0, keepdims=True)
        vn = vc * lax.rsqrt(var + LN_EPS) * rep(lng_ref, hh * hd, (hh + 1) * hd)
        stacked = jnp.concatenate([vn[:, c * LANES:(c + 1) * LANES] for c in range(nck)], axis=0)
        s = _dot(stacked.astype(BF16), sw_ref[hh]) + sb_ref[hh]
        s = jnp.concatenate([s[c * hd:(c + 1) * hd, :] for c in range(nck)], axis=1)
        parts.append(up * s)

    mix = jnp.concatenate(parts, axis=0).T.astype(BF16)
    h1 = h_ref[...] + _dot(mix, wout_ref[...])
    o_ref[...] = h1

    xn = _rms(h1, g2_ref[...])
    xn_ref[...] = xn
    xh = xn.astype(BF16)
    xl = (xn - xh.astype(F32)).astype(BF16)
    logits = _dot(xh, rwh_ref[...]) + (_dot(xh, rwl_ref[...]) + _dot(xl, rwh_ref[...]))
    lane = lax.broadcasted_iota(jnp.int32, logits.shape, 1)
    neg = jnp.float32(-jnp.inf)
    lg = jnp.where(lane < n_experts, logits, neg)
    m1 = jnp.max(lg, axis=1, keepdims=True)
    i1 = jnp.min(jnp.where(lg == m1, lane, LANES), axis=1, keepdims=True)
    lg2 = jnp.where(lane == i1, neg, lg)
    m2 = jnp.max(lg2, axis=1, keepdims=True)
    i2 = jnp.min(jnp.where(lg2 == m2, lane, LANES), axis=1, keepdims=True)
    e2 = jnp.exp(m2 - m1)
    w1 = 1.0 / (1.0 + e2)
    w2 = e2 / (1.0 + e2)
    route_ref[...] = (jnp.where(lane == ID_LANE, i1.astype(F32), 0.0)
                      + jnp.where(lane == ID_LANE + 1, i2.astype(F32), 0.0)
                      + jnp.where(lane == GATE_LANE, w1, 0.0)
                      + jnp.where(lane == GATE_LANE + 1, w2, 0.0))


def _odd_tail(zt3, yt3, h, d_b, glu_wt, glu_b_b, lng_b, sgu_wt, sgu_b, w_out, g2, router_hi,
              router_lo,
              *, layer, heads, n_experts):
    n, dm = h.shape
    cols = zt3.shape[1]
    sdim = yt3.shape[1]
    tm = min(TOKEN_TILE, n)
    nck = tm // LANES
    kern = functools.partial(_odd_tail_kernel, nck=nck, sdim=sdim, heads=heads, n_experts=n_experts)
    return pl.pallas_call(
        kern,
        out_shape=(jax.ShapeDtypeStruct((n, dm), F32),
                   jax.ShapeDtypeStruct((n, dm), F32),
                   jax.ShapeDtypeStruct((n, LANES), F32)),
        grid=(n // tm,),
        in_specs=[
            pl.BlockSpec((nck, cols, LANES), lambda i: (i, 0, 0)),
            pl.BlockSpec((nck, sdim, LANES), lambda i: (i, 0, 0)),
            pl.BlockSpec((tm, dm), lambda i: (i, 0)),
            _const_spec(d_b.shape),
            _layer_spec(glu_wt, layer),
            _const_spec(glu_b_b.shape),
            _const_spec(lng_b.shape),
            _const_spec(sgu_wt.shape),
            _const_spec(sgu_b.shape),
            _layer_spec(w_out, layer),
            _layer_spec(g2, layer),
            _const_spec(router_hi.shape),
            _const_spec(router_lo.shape),
        ],
        out_specs=(pl.BlockSpec((tm, dm), lambda i: (i, 0)),
                   pl.BlockSpec((tm, dm), lambda i: (i, 0)),
                   pl.BlockSpec((tm, LANES), lambda i: (i, 0))),
        compiler_params=_params("arbitrary"),
        name="odd_mixer_tail",
    )(zt3, yt3, h, d_b, glu_wt, glu_b_b, lng_b, sgu_wt, sgu_b, w_out, g2, router_hi, router_lo)


def _experts_kernel(te_ref, ns_ref, na_ref, x_ref, w1_ref, w3_ref, w2_ref, o_ref, *, sub):
    f = pl.program_id(1)

    @pl.when(f == 0)
    def _():
        o_ref[...] = jnp.zeros(o_ref.shape, F32)

    def swiglu_rows(start, nrows):
        rows = pl.ds(pl.multiple_of(start, sub), nrows)
        x = x_ref[rows, :].astype(BF16)
        a = _dot(x, w1_ref[0].astype(BF16))
        b = _dot(x, w3_ref[0].astype(BF16))
        act = (a * jax.nn.sigmoid(a) * b).astype(BF16)
        o_ref[rows, :] += _dot(act, w2_ref[0].astype(BF16))

    nsub = ns_ref[pl.program_id(0)]
    npair = nsub // 2

    def pair(s, carry):
        swiglu_rows(s * (2 * sub), 2 * sub)
        return carry

    lax.fori_loop(0, npair, pair, 0)

    @pl.when(nsub % 2 == 1)
    def _():
        swiglu_rows(npair * (2 * sub), sub)


def _experts(tile_expert, tile_nsub, n_active, xs, w1, w3, w2, *, layer):
    r, d = xs.shape
    fdim = w1.shape[3]
    te = EXPERT_TILE
    tf = min(EXPERT_F_TILE, fdim)
    n_tiles = r // te
    nf = fdim // tf

    def row_map(i, f, te_ref, ns_ref, na_ref):
        return (jnp.minimum(i, na_ref[0] - 1), 0)

    grid_spec = pltpu.PrefetchScalarGridSpec(
        num_scalar_prefetch=3,
        grid=(n_tiles, nf),
        in_specs=[
            pl.BlockSpec((te, d), row_map),
            pl.BlockSpec((None, 1, d, tf), lambda i, f, te_ref, *_: (layer, te_ref[i], 0, f)),
            pl.BlockSpec((None, 1, d, tf), lambda i, f, te_ref, *_: (layer, te_ref[i], 0, f)),
            pl.BlockSpec((None, 1, tf, d), lambda i, f, te_ref, *_: (layer, te_ref[i], f, 0)),
        ],
        out_specs=pl.BlockSpec((te, d), lambda i, f, *_: (i, 0)),
    )
    return pl.pallas_call(
        functools.partial(_experts_kernel, sub=min(EXPERT_SUB, te)),
        out_shape=jax.ShapeDtypeStruct((r, d), F32),
        grid_spec=grid_spec,
        compiler_params=_params("arbitrary", "arbitrary"),
        name="grouped_experts",
    )(tile_expert, tile_nsub, n_active, xs, w1, w3, w2)


def _dispatch_kernel(te_ref, off_ref, cnt_ref, cb_ref, ct_ref, o_ref, *, te, nb, n):
    i = pl.program_id(0)
    e = te_ref[i]
    row = i * te + lax.broadcasted_iota(jnp.int32, (1, te), 1)
    rank = row - off_ref[e]
    blk_end = cb_ref[0][:, LANES - 1:LANES]
    blk = jnp.sum((blk_end <= rank).astype(jnp.int32), axis=0, keepdims=True)
    blk = jnp.minimum(blk, nb - 1)
    onehot = (lax.broadcasted_iota(jnp.int32, (nb, te), 0) == blk).astype(F32)
    counts_t = jnp.dot(ct_ref[0].astype(F32), onehot, precision=lax.Precision.HIGHEST,
                       preferred_element_type=F32)
    within = jnp.sum((counts_t <= rank.astype(F32)).astype(jnp.int32), axis=0, keepdims=True)
    filler = row & (n - 1) if n & (n - 1) == 0 else lax.rem(row, n)
    o_ref[0] = jnp.where(rank < cnt_ref[e], blk * LANES + within, filler)


def _dispatch_rows(tile_expert, offs, counts, cb, ct, *, te, n):
    n_exp, nb, _ = cb.shape
    n_tiles = tile_expert.shape[0]
    grid_spec = pltpu.PrefetchScalarGridSpec(
        num_scalar_prefetch=3,
        grid=(n_tiles,),
        in_specs=[
            pl.BlockSpec((1, nb, LANES), lambda i, te_ref, *_: (te_ref[i], 0, 0)),
            pl.BlockSpec((1, LANES, nb), lambda i, te_ref, *_: (te_ref[i], 0, 0)),
        ],
        out_specs=pl.BlockSpec((1, 1, te), lambda i, *_: (i, 0, 0)),
    )
    return pl.pallas_call(
        functools.partial(_dispatch_kernel, te=te, nb=nb, n=n),
        out_shape=jax.ShapeDtypeStruct((n_tiles, 1, te), jnp.int32),
        grid_spec=grid_spec,
        compiler_params=_params("arbitrary"),
        name="dispatch_rows",
    )(tile_expert, offs, counts, cb, ct)


def _moe(h1, xn, route, w1, w3, w2, *, layer):
    n, d = h1.shape
    n_exp = w1.shape[1]
    te = EXPERT_TILE
    sub = min(EXPERT_SUB, te)
    n_tiles = (n * TOP_K + te - 1) // te + n_exp
    nb = n // LANES

    ids = route[:, ID_LANE:ID_LANE + TOP_K].astype(jnp.int32)
    chosen = ids[:, :, None] == jnp.arange(n_exp, dtype=jnp.int32)[None, None, :]
    sel = chosen.any(axis=1)
    csum = jnp.cumsum(sel.astype(jnp.int32), axis=0)
    counts = csum[-1]
    padded = ((counts + te - 1) // te) * te
    ends = jnp.cumsum(padded)
    offs = ends - padded
    tile_start = jnp.arange(n_tiles, dtype=jnp.int32) * te
    in_use = tile_start < ends[-1]
    n_active = (ends[-1] // te).astype(jnp.int32)
    tile_expert = jnp.sum(tile_start[:, None] >= ends[None, :], axis=1).astype(jnp.int32)
    last_expert = jnp.take(tile_expert, jnp.maximum(n_active - 1, 0))
    tile_expert = jnp.where(in_use, tile_expert, last_expert)
    valid = jnp.clip(counts[tile_expert] - (tile_start - offs[tile_expert]), 0, te)
    tile_nsub = jnp.where(in_use, (valid + sub - 1) // sub, 0).astype(jnp.int32)

    cb = csum.T.reshape(n_exp, nb, LANES)
    src = _dispatch_rows(tile_expert, offs.astype(jnp.int32), counts.astype(jnp.int32), cb,
                         cb.transpose(0, 2, 1), te=te, n=n)
    xs = jnp.take(xn, src.reshape(-1), axis=0, mode='clip')
    ys = _experts(tile_expert, tile_nsub, n_active.reshape(1), xs, w1, w3, w2, layer=layer)
    dest = offs[None, :] + csum - 1
    pos = jnp.sum(jnp.where(chosen, dest[:, None, :], 0), axis=2)
    return (h1, jnp.take(ys, pos[:, 0], axis=0, mode='clip'),
            jnp.take(ys, pos[:, 1], axis=0, mode='clip'), route)


def _final_norm_kernel(*refs):
    *res_refs, g_ref, o_ref = refs
    o_ref[...] = _rms(_residual(res_refs), g_ref[...])


def _final_norm(h_parts, g):
    n, d = h_parts[0].shape
    tm = min(LIGHT_TOKEN_TILE, n)
    return pl.pallas_call(
        _final_norm_kernel,
        out_shape=jax.ShapeDtypeStruct((n, d), F32),
        grid=(n // tm,),
        in_specs=_residual_specs(h_parts, tm, lambda i: (i, 0)) + [_const_spec((1, d))],
        out_specs=pl.BlockSpec((tm, d), lambda i: (i, 0)),
        compiler_params=_params("arbitrary"),
        name="final_norm",
    )(*h_parts, g)


def _even_layer(h_parts, p, *, bsz, seq):
    h = _even_mixer(h_parts, p['ev_norm1_g'], p['ev_w_in'], p['ev_conv_w'], p['ev_pool_w'],
                    p['ev_pool_scale'], p['ev_w_out'], layer=0, bsz=bsz, seq=seq)
    return _ffn(h, p['ev_norm2_g'], p['ev_ffn_w1'], p['ev_ffn_w3'], p['ev_ffn_w2'], layer=0)


def _odd_layer(h, p, moe_w, *, moe_layer, seq):
    sgu_w = p['od_sgu_w'][0]
    router_w = p['od_router_w'][0]
    heads, slen, _ = sgu_w.shape
    n_exp = router_w.shape[1]
    assert slen == LANES and seq % T_SSM == 0
    cpb = seq // T_SSM
    assert cpb & (cpb - 1) == 0, "chunks per sequence must be a power of two"

    zt3 = _odd_in(h, p['od_norm1_g'], p['od_w_in_t'], layer=0)
    tables = _s5_tables(p['od_lambda_re'], p['od_lambda_im'], p['od_log_dt'], p['od_b_re'],
                        p['od_b_im'], p['od_c_re'], p['od_c_im'], cpb=cpb)
    yt3 = _s5(zt3, *tables, layer=0, groups=p['od_lambda_re'].shape[1], cpb=cpb)

    lanes_b = lambda v: jnp.broadcast_to(v[:, None], (v.shape[0], LANES))
    cidx = jnp.arange(slen) // CHUNK
    mask = cidx[None, :] <= cidx[:, None]
    sgu_wt = jnp.where(mask[None], sgu_w, 0.0).transpose(0, 2, 1).astype(BF16)
    router_pad = jnp.zeros((router_w.shape[0], LANES), F32).at[:, :n_exp].set(router_w)
    router_hi = router_pad.astype(BF16)
    router_lo = (router_pad - router_hi.astype(F32)).astype(BF16)
    h1, xn, route = _odd_tail(
        zt3, yt3, h, lanes_b(p['od_d'][0]), p['od_glu_wt'], lanes_b(p['od_glu_b'][0]),
        lanes_b(p['od_sgu_ln_g'][0]), sgu_wt, p['od_sgu_b'][0][:, None, :], p['od_w_out'],
        p['od_norm2_g'], router_hi, router_lo, layer=0, heads=heads, n_experts=n_exp)
    return _moe(h1, xn, route, *moe_w, layer=moe_layer)


def kernel(x, ev_norm1_g, ev_w_in, ev_conv_w, ev_pool_w, ev_pool_scale, ev_w_out, ev_norm2_g, ev_ffn_w1, ev_ffn_w3, ev_ffn_w2, od_norm1_g, od_w_in, od_lambda_re, od_lambda_im, od_log_dt, od_b_re, od_b_im, od_c_re, od_c_im, od_d, od_glu_w, od_glu_b, od_sgu_ln_g, od_sgu_w, od_sgu_b, od_w_out, od_norm2_g, od_router_w, od_moe_w1, od_moe_w3, od_moe_w2, final_norm_g):
    bsz, seq, d = x.shape
    depth = ev_norm1_g.shape[0] + od_norm1_g.shape[0]
    row = lambda v: v[:, None, :]

    def even_params(i):
        one = lambda v: v[i:i + 1]
        return dict(
            ev_norm1_g=row(one(ev_norm1_g)), ev_w_in=one(ev_w_in).astype(BF16),
            ev_conv_w=one(ev_conv_w), ev_pool_w=one(ev_pool_w).astype(BF16),
            ev_pool_scale=row(one(ev_pool_scale)), ev_w_out=one(ev_w_out).astype(BF16),
            ev_norm2_g=row(one(ev_norm2_g)), ev_ffn_w1=one(ev_ffn_w1).astype(BF16),
            ev_ffn_w3=one(ev_ffn_w3).astype(BF16), ev_ffn_w2=one(ev_ffn_w2).astype(BF16))

    def odd_params(i):
        one = lambda v: v[i:i + 1]
        return dict(
            od_norm1_g=row(one(od_norm1_g)), od_w_in_t=one(od_w_in).transpose(0, 2, 1).astype(BF16),
            od_lambda_re=one(od_lambda_re), od_lambda_im=one(od_lambda_im),
            od_log_dt=one(od_log_dt), od_b_re=one(od_b_re), od_b_im=one(od_b_im),
            od_c_re=one(od_c_re), od_c_im=one(od_c_im), od_d=one(od_d),
            od_glu_wt=one(od_glu_w).transpose(0, 2, 1).astype(BF16), od_glu_b=one(od_glu_b),
            od_sgu_ln_g=one(od_sgu_ln_g), od_sgu_w=one(od_sgu_w), od_sgu_b=one(od_sgu_b),
            od_w_out=one(od_w_out).astype(BF16), od_norm2_g=row(one(od_norm2_g)),
            od_router_w=one(od_router_w))

    moe_w = (od_moe_w1, od_moe_w3, od_moe_w2)
    h_parts = (x.reshape(bsz * seq, d),)
    for layer in range(depth):
        if layer % 2 == 0:
            h_parts = (_even_layer(h_parts, even_params(layer // 2), bsz=bsz, seq=seq),)
        else:
            assert len(h_parts) == 1
            h_parts = _odd_layer(h_parts[0], odd_params(layer // 2), moe_w,
                                 moe_layer=layer // 2, seq=seq)
    return _final_norm(h_parts, final_norm_g[None]).reshape(bsz, seq, d)
```

```python
import functools
import math

import jax
import jax.numpy as jnp
from jax import lax
from jax.experimental import pallas as pl
from jax.experimental.pallas import tpu as pltpu

F32 = jnp.float32
BF16 = jnp.bfloat16

RMS_EPS = 1e-5
LN_EPS = 1e-5
CHUNK = 64
POOL_WINDOWS = (2, 4, 8, 16)
CONV_WIDTH = 3
TOP_K = 2

LANES = 128
V7X_VMEM_BYTES = 64 * 1024 * 1024
VMEM_LIMIT = V7X_VMEM_BYTES * 7 // 8
HALO = 16
T_SSM = LANES
TOKEN_TILE = 512
LIGHT_TOKEN_TILE = 1024
EVEN_ROW_BLOCKS = 2
ID_LANE = LANES - 4
GATE_LANE = LANES - 2
EXPERT_TILE = 1536
EXPERT_SUB = 512
EXPERT_F_TILE = 512
FFN_F_TILE = 1024


def _const_spec(shape):
    nd = len(shape)
    return pl.BlockSpec(shape, lambda *_: (0,) * nd, pipeline_mode=pl.Buffered(1))


def _layer_spec(arr, layer):
    nd = arr.ndim - 1
    return pl.BlockSpec((None,) + arr.shape[1:], lambda *_: (layer,) + (0,) * nd,
                        pipeline_mode=pl.Buffered(1))


def _params(*sem):
    return pltpu.CompilerParams(dimension_semantics=sem, vmem_limit_bytes=VMEM_LIMIT)


def _rms(x, g):
    return x * lax.rsqrt(jnp.mean(x * x, axis=-1, keepdims=True) + RMS_EPS) * g


def _dot(a, b):
    return jnp.dot(a, b, preferred_element_type=F32)


def _static_loop(n, body):
    for j in range(n):
        body(j)


def _residual(refs, rows=slice(None)):
    if len(refs) == 1:
        return refs[0][rows, :]
    h_ref, ya_ref, yb_ref, route_ref = refs
    route = route_ref[rows, :]
    ga = route[:, GATE_LANE:GATE_LANE + 1]
    gb = route[:, GATE_LANE + 1:GATE_LANE + 2]
    return h_ref[rows, :] + ga * ya_ref[rows, :] + gb * yb_ref[rows, :]


def _residual_specs(h_parts, tm, index_map):
    return [pl.BlockSpec((tm, p.shape[1]), index_map) for p in h_parts]


def _even_mixer_kernel(*refs, tm, cdim, nres):
    res_refs = refs[:nres]
    g_ref, win_ref, cw_ref, pw_ref, ps_ref, wout_ref, o_ref, zc_ext, p_ext = refs[nres:]
    j = pl.program_id(1)

    @pl.when(j == 0)
    def _():
        zc_ext[0:HALO, :] = jnp.zeros((HALO, cdim), F32)
        p_ext[0:HALO, :] = jnp.zeros((HALO, cdim), F32)

    rb = tm // EVEN_ROW_BLOCKS
    for q in range(EVEN_ROW_BLOCKS):
        r0 = q * rb
        e0 = HALO + r0
        h = _residual(res_refs, slice(r0, r0 + rb))
        hn = _rms(h, g_ref[...]).astype(BF16)
        z = _dot(hn, win_ref[...])
        gb = z[:, 0:cdim]
        zc = z[:, cdim:2 * cdim] * z[:, 2 * cdim:3 * cdim]
        p = z[:, 3 * cdim:4 * cdim]
        zc_ext[e0:e0 + rb, :] = zc
        p_ext[e0:e0 + rb, :] = p

        y = cw_ref[CONV_WIDTH - 1:CONV_WIDTH, :] * zc
        for k in range(CONV_WIDTH - 1):
            sh = CONV_WIDTH - 1 - k
            y = y + cw_ref[k:k + 1, :] * zc_ext[e0 - sh:e0 - sh + rb, :]
        parts = [(gb * y).astype(BF16)]

        pos = (j * tm + r0 + 1 + lax.broadcasted_iota(jnp.int32, (rb, LANES), 0)).astype(F32)
        for gi, w in enumerate(POOL_WINDOWS):
            c0 = gi * LANES
            pg = p[:, c0:c0 + LANES]
            s = pg
            for k in range(1, w):
                s = s + p_ext[e0 - k:e0 - k + rb, c0:c0 + LANES]
            pooled = s / jnp.minimum(pos, float(w)) - pg
            mixed = _dot(pooled.astype(BF16), pw_ref[gi]) * ps_ref[:, c0:c0 + LANES]
            parts.append(mixed.astype(BF16))

        mix = jnp.concatenate(parts, axis=1)
        o_ref[r0:r0 + rb, :] = h + _dot(mix, wout_ref[...])

    zc_ext[0:HALO, :] = zc_ext[tm:tm + HALO, :]
    p_ext[0:HALO, :] = p_ext[tm:tm + HALO, :]


def _even_mixer(h_parts, g, w_in, conv_w, pool_w, pool_scale, w_out, *, layer, bsz, seq):
    n, d = h_parts[0].shape
    cdim = w_in.shape[2] // 4
    tm = min(TOKEN_TILE, seq)
    nj = seq // tm
    kern = functools.partial(_even_mixer_kernel, tm=tm, cdim=cdim, nres=len(h_parts))
    return pl.pallas_call(
        kern,
        out_shape=jax.ShapeDtypeStruct((n, d), F32),
        grid=(bsz, nj),
        in_specs=_residual_specs(h_parts, tm, lambda b, j: (b * nj + j, 0)) + [
            _layer_spec(g, layer),
            _layer_spec(w_in, layer),
            _layer_spec(conv_w, layer),
            _layer_spec(pool_w, layer),
            _layer_spec(pool_scale, layer),
            _layer_spec(w_out, layer),
        ],
        out_specs=pl.BlockSpec((tm, d), lambda b, j: (b * nj + j, 0)),
        scratch_shapes=[pltpu.VMEM((HALO + tm, cdim), F32), pltpu.VMEM((HALO + tm, cdim), F32)],
        compiler_params=_params("arbitrary", "arbitrary"),
        name="even_mixer",
    )(*h_parts, g, w_in, conv_w, pool_w, pool_scale, w_out)


def _ffn_kernel(h_ref, g_ref, w1_ref, w3_ref, w2_ref, o_ref, *, fdim):
    h = h_ref[...]
    hn = _rms(h, g_ref[...]).astype(BF16)
    acc = h
    for f0 in range(0, fdim, FFN_F_TILE):
        f1 = min(f0 + FFN_F_TILE, fdim)
        a = _dot(hn, w1_ref[:, f0:f1])
        b = _dot(hn, w3_ref[:, f0:f1])
        act = (a * jax.nn.sigmoid(a) * b).astype(BF16)
        acc = acc + _dot(act, w2_ref[f0:f1, :])
    o_ref[...] = acc


def _ffn(h, g, w1, w3, w2, *, layer):
    n, d = h.shape
    fdim = w1.shape[2]
    tm = min(TOKEN_TILE, n)
    return pl.pallas_call(
        functools.partial(_ffn_kernel, fdim=fdim),
        out_shape=jax.ShapeDtypeStruct((n, d), F32),
        grid=(n // tm,),
        in_specs=[
            pl.BlockSpec((tm, d), lambda i: (i, 0)),
            _layer_spec(g, layer),
            _layer_spec(w1, layer),
            _layer_spec(w3, layer),
            _layer_spec(w2, layer),
        ],
        out_specs=pl.BlockSpec((tm, d), lambda i: (i, 0)),
        compiler_params=_params("arbitrary"),
        name="dense_swiglu",
    )(h, g, w1, w3, w2)


def _odd_in_kernel(h_ref, g_ref, wt_ref, o_ref, *, nck):
    hn = _rms(h_ref[...], g_ref[...]).astype(BF16)
    zt = lax.dot_general(wt_ref[...], hn, (((1,), (1,)), ((), ())), preferred_element_type=F32)
    for c in range(nck):
        o_ref[c] = zt[:, c * LANES:(c + 1) * LANES]


def _odd_in(h, g, w_in_t, *, layer):
    n, d = h.shape
    cols = w_in_t.shape[1]
    tm = min(LIGHT_TOKEN_TILE, n)
    nck = tm // LANES
    return pl.pallas_call(
        functools.partial(_odd_in_kernel, nck=nck),
        out_shape=jax.ShapeDtypeStruct((n // LANES, cols, LANES), F32),
        grid=(n // tm,),
        in_specs=[
            pl.BlockSpec((tm, d), lambda i: (i, 0)),
            _layer_spec(g, layer),
            _layer_spec(w_in_t, layer),
        ],
        out_specs=pl.BlockSpec((nck, cols, LANES), lambda i: (i, 0, 0)),
        compiler_params=_params("arbitrary"),
        name="odd_in_proj",
    )(h, g, w_in_t)


def _s5_kernel(u_ref, kt_ref, ws_ref, wc_ref, coef_ref, y_ref, m_scr, *, hdim, cpb, n_steps):
    t = T_SSM
    row = lax.broadcasted_iota(jnp.int32, (t, t), 0)
    col = lax.broadcasted_iota(jnp.int32, (t, t), 1)
    causal = col >= row
    def expand_row_block(j):
        r0 = j * t if isinstance(j, int) else pl.multiple_of(j * t, t)
        for i in range(hdim):
            kb = jnp.broadcast_to(kt_ref[0, j, i:i + 1, :], (t, t))
            tz = pltpu.roll(kb, 0, 1, stride=1, stride_axis=0)
            m_scr[pl.ds(r0, t), i * t:(i + 1) * t] = jnp.where(causal, tz, 0.0).astype(BF16)

    _static_loop(hdim, expand_row_block)

    x = jnp.concatenate([u_ref[:, c, :] for c in range(hdim)], axis=1).astype(BF16)
    y = _dot(x, m_scr[...])
    st = _dot(x, ws_ref[0])
    half = st.shape[1] // 2
    rb = lax.broadcasted_iota(jnp.int32, st.shape, 0) & (cpb - 1)
    for k in range(n_steps):
        d = 1 << k
        sh = jnp.where(rb >= d, pltpu.roll(st, d, 0), 0.0)
        sw = pltpu.roll(sh, half, 1)
        st = st + coef_ref[0, k, 0:1, :] * sh + coef_ref[0, k, 1:2, :] * sw
    prev = jnp.where(rb >= 1, pltpu.roll(st, 1, 0), 0.0)
    wc = jnp.concatenate([wc_ref[0, i] for i in range(hdim)], axis=1)
    y = y + _dot(prev.astype(BF16), wc)
    for i in range(hdim):
        y_ref[:, i, :] = y[:, i * t:(i + 1) * t]


def _s5(zt3, kt, ws, wc, coef, *, layer, groups, cpb):
    nc = zt3.shape[0]
    g, hdim = groups, kt.shape[1]
    n_steps = coef.shape[1]
    g0 = layer * groups
    return pl.pallas_call(
        functools.partial(_s5_kernel, hdim=hdim, cpb=cpb, n_steps=n_steps),
        out_shape=jax.ShapeDtypeStruct((nc, g * hdim, T_SSM), F32),
        grid=(g,),
        in_specs=[
            pl.BlockSpec((nc, hdim, T_SSM), lambda i: (0, i, 0)),
            pl.BlockSpec((1,) + kt.shape[1:], lambda i: (g0 + i, 0, 0, 0)),
            pl.BlockSpec((1,) + ws.shape[1:], lambda i: (g0 + i, 0, 0)),
            pl.BlockSpec((1,) + wc.shape[1:], lambda i: (g0 + i, 0, 0, 0)),
            pl.BlockSpec((1,) + coef.shape[1:], lambda i: (g0 + i, 0, 0, 0)),
        ],
        out_specs=pl.BlockSpec((nc, hdim, T_SSM), lambda i: (0, i, 0)),
        scratch_shapes=[pltpu.VMEM((hdim * T_SSM, hdim * T_SSM), BF16)],
        compiler_params=_params("arbitrary"),
        name="s5_chunked",
    )(zt3, kt, ws, wc, coef)


def _s5_tables(lam_re, lam_im, log_dt, b_re, b_im, c_re, c_im, *, cpb):
    flat = lambda v: v.reshape((-1,) + v.shape[2:])
    cat = lambda u, v: jnp.concatenate([u, v], axis=-1)
    lr, li, log_dt = flat(lam_re), flat(lam_im), flat(log_dt)
    b_re, b_im, c_re, c_im = flat(b_re), flat(b_im), flat(c_re), flat(c_im)
    gg, p = lr.shape
    hdim = b_re.shape[2]
    t = T_SSM
    dt = jnp.exp(log_dt)[:, None]
    mag = jnp.exp(lr * dt)
    abar_r = mag * jnp.cos(li * dt)
    abar_i = mag * jnp.sin(li * dt)
    qr, qi = abar_r - 1.0, abar_i
    den = lr * lr + li * li
    fr = ((qr * lr + qi * li) / den)[..., None]
    fi = ((qi * lr - qr * li) / den)[..., None]
    bjr = (fr * b_re - fi * b_im).transpose(0, 2, 1)
    bji = (fr * b_im + fi * b_re).transpose(0, 2, 1)

    def apow(k):
        kk = k[None, :, None]
        m = jnp.exp(kk * (lr * dt)[:, None, :])
        th = kk * (li * dt)[:, None, :]
        return m * jnp.cos(th), m * jnp.sin(th)

    par, pai = apow(jnp.arange(t + 1, dtype=F32))
    pr, pi = par[:, :t], pai[:, :t]
    p1r, p1i = par[:, 1:], pai[:, 1:]
    cbr = c_re[:, None] * bjr[:, :, None, :] - c_im[:, None] * bji[:, :, None, :]
    cbi = c_re[:, None] * bji[:, :, None, :] + c_im[:, None] * bjr[:, :, None, :]
    kt = jnp.einsum('gjic,glc->gjil', cat(cbr, -cbi), cat(pr, pi),
                    precision=lax.Precision.HIGHEST)
    er, ei = apow(jnp.arange(t - 1, -1, -1, dtype=F32))
    ws = (cat(er, er)[:, None] * cat(bjr, bji)[:, :, None, :]
          + cat(-ei, ei)[:, None] * cat(bji, bjr)[:, :, None, :])
    ws = ws.astype(BF16).reshape(gg, hdim * t, 2 * p)
    p1rt = p1r.transpose(0, 2, 1)
    p1it = p1i.transpose(0, 2, 1)
    wc = (cat(c_re, -c_im)[..., None] * jnp.concatenate([p1rt, p1rt], axis=1)[:, None]
          + cat(-c_im, -c_re)[..., None] * jnp.concatenate([p1it, p1it], axis=1)[:, None])
    wc = wc.astype(BF16)
    n_steps = max(1, int(math.log2(cpb)))
    sr, si = apow(float(t) * (2.0 ** jnp.arange(n_steps, dtype=F32)))
    coef = jnp.stack([cat(sr, sr), cat(-si, si)], axis=2)
    return kt, ws, wc, coef


def _odd_tail_kernel(z_ref, y_ref, h_ref, d_ref, gw_ref, gb_ref, lng_ref, sw_ref, sb_ref,
                     wout_ref, g2_ref, rwh_ref, rwl_ref, o_ref, xn_ref, route_ref,
                     *, nck, sdim, heads, n_experts):
    def cat(ref, r0, r1):
        return jnp.concatenate([ref[c, r0:r1, :] for c in range(nck)], axis=1)

    def rep(ref, r0=None, r1=None):
        v = ref[...] if r0 is None else ref[r0:r1, :]
        return jnp.concatenate([v] * nck, axis=1)

    ut = cat(z_ref, 0, sdim)
    yt = cat(y_ref, 0, sdim)
    gt = jax.nn.gelu(yt + rep(d_ref) * ut)
    glu = _dot(gw_ref[...], gt.astype(BF16)) + rep(gb_ref)
    parts = [gt * jax.nn.sigmoid(glu)]

    hd = sdim // heads
    for hh in range(heads):
        r0 = sdim + hh * hd
        up = jax.nn.gelu(cat(z_ref, r0, r0 + hd))
        vp = jax.nn.gelu(cat(z_ref, sdim + r0, sdim + r0 + hd))
        mu = jnp.mean(vp, axis=0, keepdims=True)
        vc = vp - mu
        var = jnp.mean(vc * vc, axis=0, keepdims=True)
        vn = vc * lax.rsqrt(var + LN_EPS) * rep(lng_ref, hh * hd, (hh + 1) * hd)
        stacked = jnp.concatenate([vn[:, c * LANES:(c + 1) * LANES] for c in range(nck)], axis=0)
        s = _dot(stacked.astype(BF16), sw_ref[hh]) + sb_ref[hh]
        s = jnp.concatenate([s[c * hd:(c + 1) * hd, :] for c in range(nck)], axis=1)
        parts.append(up * s)

    mix = jnp.concatenate(parts, axis=0).T.astype(BF16)
    h1 = h_ref[...] + _dot(mix, wout_ref[...])
    o_ref[...] = h1

    xn = _rms(h1, g2_ref[...])
    xn_ref[...] = xn
    xh = xn.astype(BF16)
    xl = (xn - xh.astype(F32)).astype(BF16)
    logits = _dot(xh, rwh_ref[...]) + (_dot(xh, rwl_ref[...]) + _dot(xl, rwh_ref[...]))
    lane = lax.broadcasted_iota(jnp.int32, logits.shape, 1)
    neg = jnp.float32(-jnp.inf)
    lg = jnp.where(lane < n_experts, logits, neg)
    m1 = jnp.max(lg, axis=1, keepdims=True)
    i1 = jnp.min(jnp.where(lg == m1, lane, LANES), axis=1, keepdims=True)
    lg2 = jnp.where(lane == i1, neg, lg)
    m2 = jnp.max(lg2, axis=1, keepdims=True)
    i2 = jnp.min(jnp.where(lg2 == m2, lane, LANES), axis=1, keepdims=True)
    e2 = jnp.exp(m2 - m1)
    w1 = 1.0 / (1.0 + e2)
    w2 = e2 / (1.0 + e2)
    route_ref[...] = (jnp.where(lane == ID_LANE, i1.astype(F32), 0.0)
                      + jnp.where(lane == ID_LANE + 1, i2.astype(F32), 0.0)
                      + jnp.where(lane == GATE_LANE, w1, 0.0)
                      + jnp.where(lane == GATE_LANE + 1, w2, 0.0))


def _odd_tail(zt3, yt3, h, d_b, glu_wt, glu_b_b, lng_b, sgu_wt, sgu_b, w_out, g2, router_hi,
              router_lo,
              *, layer, heads, n_experts):
    n, dm = h.shape
    cols = zt3.shape[1]
    sdim = yt3.shape[1]
    tm = min(TOKEN_TILE, n)
    nck = tm // LANES
    kern = functools.partial(_odd_tail_kernel, nck=nck, sdim=sdim, heads=heads, n_experts=n_experts)
    return pl.pallas_call(
        kern,
        out_shape=(jax.ShapeDtypeStruct((n, dm), F32),
                   jax.ShapeDtypeStruct((n, dm), F32),
                   jax.ShapeDtypeStruct((n, LANES), F32)),
        grid=(n // tm,),
        in_specs=[
            pl.BlockSpec((nck, cols, LANES), lambda i: (i, 0, 0)),
            pl.BlockSpec((nck, sdim, LANES), lambda i: (i, 0, 0)),
            pl.BlockSpec((tm, dm), lambda i: (i, 0)),
            _const_spec(d_b.shape),
            _layer_spec(glu_wt, layer),
            _const_spec(glu_b_b.shape),
            _const_spec(lng_b.shape),
            _const_spec(sgu_wt.shape),
            _const_spec(sgu_b.shape),
            _layer_spec(w_out, layer),
            _layer_spec(g2, layer),
            _const_spec(router_hi.shape),
            _const_spec(router_lo.shape),
        ],
        out_specs=(pl.BlockSpec((tm, dm), lambda i: (i, 0)),
                   pl.BlockSpec((tm, dm), lambda i: (i, 0)),
                   pl.BlockSpec((tm, LANES), lambda i: (i, 0))),
        compiler_params=_params("arbitrary"),
        name="odd_mixer_tail",
    )(zt3, yt3, h, d_b, glu_wt, glu_b_b, lng_b, sgu_wt, sgu_b, w_out, g2, router_hi, router_lo)


def _experts_kernel(te_ref, ns_ref, na_ref, x_ref, w1_ref, w3_ref, w2_ref, o_ref, *, sub):
    f = pl.program_id(1)

    @pl.when(f == 0)
    def _():
        o_ref[...] = jnp.zeros(o_ref.shape, F32)

    def swiglu_rows(start, nrows):
        rows = pl.ds(pl.multiple_of(start, sub), nrows)
        x = x_ref[rows, :].astype(BF16)
        a = _dot(x, w1_ref[0].astype(BF16))
        b = _dot(x, w3_ref[0].astype(BF16))
        act = (a * jax.nn.sigmoid(a) * b).astype(BF16)
        o_ref[rows, :] += _dot(act, w2_ref[0].astype(BF16))

    nsub = ns_ref[pl.program_id(0)]
    npair = nsub // 2

    def pair(s, carry):
        swiglu_rows(s * (2 * sub), 2 * sub)
        return carry

    lax.fori_loop(0, npair, pair, 0)

    @pl.when(nsub % 2 == 1)
    def _():
        swiglu_rows(npair * (2 * sub), sub)


def _experts(tile_expert, tile_nsub, n_active, xs, w1, w3, w2, *, layer):
    r, d = xs.shape
    fdim = w1.shape[3]
    te = EXPERT_TILE
    tf = min(EXPERT_F_TILE, fdim)
    n_tiles = r // te
    nf = fdim // tf

    def row_map(i, f, te_ref, ns_ref, na_ref):
        return (jnp.minimum(i, na_ref[0] - 1), 0)

    grid_spec = pltpu.PrefetchScalarGridSpec(
        num_scalar_prefetch=3,
        grid=(n_tiles, nf),
        in_specs=[
            pl.BlockSpec((te, d), row_map),
            pl.BlockSpec((None, 1, d, tf), lambda i, f, te_ref, *_: (layer, te_ref[i], 0, f)),
            pl.BlockSpec((None, 1, d, tf), lambda i, f, te_ref, *_: (layer, te_ref[i], 0, f)),
            pl.BlockSpec((None, 1, tf, d), lambda i, f, te_ref, *_: (layer, te_ref[i], f, 0)),
        ],
        out_specs=pl.BlockSpec((te, d), lambda i, f, *_: (i, 0)),
    )
    return pl.pallas_call(
        functools.partial(_experts_kernel, sub=min(EXPERT_SUB, te)),
        out_shape=jax.ShapeDtypeStruct((r, d), F32),
        grid_spec=grid_spec,
        compiler_params=_params("arbitrary", "arbitrary"),
        name="grouped_experts",
    )(tile_expert, tile_nsub, n_active, xs, w1, w3, w2)


def _dispatch_kernel(te_ref, off_ref, cnt_ref, cb_ref, ct_ref, o_ref, *, te, nb, n):
    i = pl.program_id(0)
    e = te_ref[i]
    row = i * te + lax.broadcasted_iota(jnp.int32, (1, te), 1)
    rank = row - off_ref[e]
    blk_end = cb_ref[0][:, LANES - 1:LANES]
    blk = jnp.sum((blk_end <= rank).astype(jnp.int32), axis=0, keepdims=True)
    blk = jnp.minimum(blk, nb - 1)
    onehot = (lax.broadcasted_iota(jnp.int32, (nb, te), 0) == blk).astype(F32)
    counts_t = jnp.dot(ct_ref[0].astype(F32), onehot, precision=lax.Precision.HIGHEST,
                       preferred_element_type=F32)
    within = jnp.sum((counts_t <= rank.astype(F32)).astype(jnp.int32), axis=0, keepdims=True)
    filler = row & (n - 1) if n & (n - 1) == 0 else lax.rem(row, n)
    o_ref[0] = jnp.where(rank < cnt_ref[e], blk * LANES + within, filler)


def _dispatch_rows(tile_expert, offs, counts, cb, ct, *, te, n):
    n_exp, nb, _ = cb.shape
    n_tiles = tile_expert.shape[0]
    grid_spec = pltpu.PrefetchScalarGridSpec(
        num_scalar_prefetch=3,
        grid=(n_tiles,),
        in_specs=[
            pl.BlockSpec((1, nb, LANES), lambda i, te_ref, *_: (te_ref[i], 0, 0)),
            pl.BlockSpec((1, LANES, nb), lambda i, te_ref, *_: (te_ref[i], 0, 0)),
        ],
        out_specs=pl.BlockSpec((1, 1, te), lambda i, *_: (i, 0, 0)),
    )
    return pl.pallas_call(
        functools.partial(_dispatch_kernel, te=te, nb=nb, n=n),
        out_shape=jax.ShapeDtypeStruct((n_tiles, 1, te), jnp.int32),
        grid_spec=grid_spec,
        compiler_params=_params("arbitrary"),
        name="dispatch_rows",
    )(tile_expert, offs, counts, cb, ct)


def _moe(h1, xn, route, w1, w3, w2, *, layer):
    n, d = h1.shape
    n_exp = w1.shape[1]
    te = EXPERT_TILE
    sub = min(EXPERT_SUB, te)
    n_tiles = (n * TOP_K + te - 1) // te + n_exp
    nb = n // LANES

    ids = route[:, ID_LANE:ID_LANE + TOP_K].astype(jnp.int32)
    chosen = ids[:, :, None] == jnp.arange(n_exp, dtype=jnp.int32)[None, None, :]
    sel = chosen.any(axis=1)
    csum = jnp.cumsum(sel.astype(jnp.int32), axis=0)
    counts = csum[-1]
    padded = ((counts + te - 1) // te) * te
    ends = jnp.cumsum(padded)
    offs = ends - padded
    tile_start = jnp.arange(n_tiles, dtype=jnp.int32) * te
    in_use = tile_start < ends[-1]
    n_active = (ends[-1] // te).astype(jnp.int32)
    tile_expert = jnp.sum(tile_start[:, None] >= ends[None, :], axis=1).astype(jnp.int32)
    last_expert = jnp.take(tile_expert, jnp.maximum(n_active - 1, 0))
    tile_expert = jnp.where(in_use, tile_expert, last_expert)
    valid = jnp.clip(counts[tile_expert] - (tile_start - offs[tile_expert]), 0, te)
    tile_nsub = jnp.where(in_use, (valid + sub - 1) // sub, 0).astype(jnp.int32)

    cb = csum.T.reshape(n_exp, nb, LANES)
    src = _dispatch_rows(tile_expert, offs.astype(jnp.int32), counts.astype(jnp.int32), cb,
                         cb.transpose(0, 2, 1), te=te, n=n)
    xs = jnp.take(xn, src.reshape(-1), axis=0, mode='clip')
    ys = _experts(tile_expert, tile_nsub, n_active.reshape(1), xs, w1, w3, w2, layer=layer)
    dest = offs[None, :] + csum - 1
    pos = jnp.sum(jnp.where(chosen, dest[:, None, :], 0), axis=2)
    return (h1, jnp.take(ys, pos[:, 0], axis=0, mode='clip'),
            jnp.take(ys, pos[:, 1], axis=0, mode='clip'), route)


def _final_norm_kernel(*refs):
    *res_refs, g_ref, o_ref = refs
    o_ref[...] = _rms(_residual(res_refs), g_ref[...])


def _final_norm(h_parts, g):
    n, d = h_parts[0].shape
    tm = min(LIGHT_TOKEN_TILE, n)
    return pl.pallas_call(
        _final_norm_kernel,
        out_shape=jax.ShapeDtypeStruct((n, d), F32),
        grid=(n // tm,),
        in_specs=_residual_specs(h_parts, tm, lambda i: (i, 0)) + [_const_spec((1, d))],
        out_specs=pl.BlockSpec((tm, d), lambda i: (i, 0)),
        compiler_params=_params("arbitrary"),
        name="final_norm",
    )(*h_parts, g)


def _even_layer(h_parts, p, *, bsz, seq):
    h = _even_mixer(h_parts, p['ev_norm1_g'], p['ev_w_in'], p['ev_conv_w'], p['ev_pool_w'],
                    p['ev_pool_scale'], p['ev_w_out'], layer=0, bsz=bsz, seq=seq)
    return _ffn(h, p['ev_norm2_g'], p['ev_ffn_w1'], p['ev_ffn_w3'], p['ev_ffn_w2'], layer=0)


def _odd_layer(h, p, moe_w, *, moe_layer, seq):
    sgu_w = p['od_sgu_w'][0]
    router_w = p['od_router_w'][0]
    heads, slen, _ = sgu_w.shape
    n_exp = router_w.shape[1]
    assert slen == LANES and seq % T_SSM == 0
    cpb = seq // T_SSM
    assert cpb & (cpb - 1) == 0, "chunks per sequence must be a power of two"

    zt3 = _odd_in(h, p['od_norm1_g'], p['od_w_in_t'], layer=0)
    tables = _s5_tables(p['od_lambda_re'], p['od_lambda_im'], p['od_log_dt'], p['od_b_re'],
                        p['od_b_im'], p['od_c_re'], p['od_c_im'], cpb=cpb)
    yt3 = _s5(zt3, *tables, layer=0, groups=p['od_lambda_re'].shape[1], cpb=cpb)

    lanes_b = lambda v: jnp.broadcast_to(v[:, None], (v.shape[0], LANES))
    cidx = jnp.arange(slen) // CHUNK
    mask = cidx[None, :] <= cidx[:, None]
    sgu_wt = jnp.where(mask[None], sgu_w, 0.0).transpose(0, 2, 1).astype(BF16)
    router_pad = jnp.zeros((router_w.shape[0], LANES), F32).at[:, :n_exp].set(router_w)
    router_hi = router_pad.astype(BF16)
    router_lo = (router_pad - router_hi.astype(F32)).astype(BF16)
    h1, xn, route = _odd_tail(
        zt3, yt3, h, lanes_b(p['od_d'][0]), p['od_glu_wt'], lanes_b(p['od_glu_b'][0]),
        lanes_b(p['od_sgu_ln_g'][0]), sgu_wt, p['od_sgu_b'][0][:, None, :], p['od_w_out'],
        p['od_norm2_g'], router_hi, router_lo, layer=0, heads=heads, n_experts=n_exp)
    return _moe(h1, xn, route, *moe_w, layer=moe_layer)


def kernel(x, ev_norm1_g, ev_w_in, ev_conv_w, ev_pool_w, ev_pool_scale, ev_w_out, ev_norm2_g, ev_ffn_w1, ev_ffn_w3, ev_ffn_w2, od_norm1_g, od_w_in, od_lambda_re, od_lambda_im, od_log_dt, od_b_re, od_b_im, od_c_re, od_c_im, od_d, od_glu_w, od_glu_b, od_sgu_ln_g, od_sgu_w, od_sgu_b, od_w_out, od_norm2_g, od_router_w, od_moe_w1, od_moe_w3, od_moe_w2, final_norm_g):
    bsz, seq, d = x.shape
    depth = ev_norm1_g.shape[0] + od_norm1_g.shape[0]
    row = lambda v: v[:, None, :]

    def even_params(i):
        one = lambda v: v[i:i + 1]
        return dict(
            ev_norm1_g=row(one(ev_norm1_g)), ev_w_in=one(ev_w_in).astype(BF16),
            ev_conv_w=one(ev_conv_w), ev_pool_w=one(ev_pool_w).astype(BF16),
            ev_pool_scale=row(one(ev_pool_scale)), ev_w_out=one(ev_w_out).astype(BF16),
            ev_norm2_g=row(one(ev_norm2_g)), ev_ffn_w1=one(ev_ffn_w1).astype(BF16),
            ev_ffn_w3=one(ev_ffn_w3).astype(BF16), ev_ffn_w2=one(ev_ffn_w2).astype(BF16))

    def odd_params(i):
        one = lambda v: v[i:i + 1]
        return dict(
            od_norm1_g=row(one(od_norm1_g)), od_w_in_t=one(od_w_in).transpose(0, 2, 1).astype(BF16),
            od_lambda_re=one(od_lambda_re), od_lambda_im=one(od_lambda_im),
            od_log_dt=one(od_log_dt), od_b_re=one(od_b_re), od_b_im=one(od_b_im),
            od_c_re=one(od_c_re), od_c_im=one(od_c_im), od_d=one(od_d),
            od_glu_wt=one(od_glu_w).transpose(0, 2, 1).astype(BF16), od_glu_b=one(od_glu_b),
            od_sgu_ln_g=one(od_sgu_ln_g), od_sgu_w=one(od_sgu_w), od_sgu_b=one(od_sgu_b),
            od_w_out=one(od_w_out).astype(BF16), od_norm2_g=row(one(od_norm2_g)),
            od_router_w=one(od_router_w))

    moe_w = (od_moe_w1, od_moe_w3, od_moe_w2)
    h_parts = (x.reshape(bsz * seq, d),)
    for layer in range(depth):
        if layer % 2 == 0:
            h_parts = (_even_layer(h_parts, even_params(layer // 2), bsz=bsz, seq=seq),)
        else:
            assert len(h_parts) == 1
            h_parts = _odd_layer(h_parts[0], odd_params(layer // 2), moe_w,
                                 moe_layer=layer // 2, seq=seq)
    return _final_norm(h_parts, final_norm_g[None]).reshape(bsz, seq, d)
```

```python
import functools
import math

import jax
import jax.numpy as jnp
from jax import lax
from jax.experimental import pallas as pl
from jax.experimental.pallas import tpu as pltpu

F32 = jnp.float32
BF16 = jnp.bfloat16

RMS_EPS = 1e-5
LN_EPS = 1e-5
CHUNK = 64
POOL_WINDOWS = (2, 4, 8, 16)
CONV_WIDTH = 3
TOP_K = 2

LANES = 128
V7X_VMEM_BYTES = 64 * 1024 * 1024
VMEM_LIMIT = V7X_VMEM_BYTES * 7 // 8
HALO = 16
T_SSM = LANES
TOKEN_TILE = 512
LIGHT_TOKEN_TILE = 1024
EVEN_ROW_BLOCKS = 2
ID_LANE = LANES - 4
GATE_LANE = LANES - 2
EXPERT_TILE = 2048
EXPERT_SUB = 512
EXPERT_F_TILE = 512
FFN_F_TILE = 1024


def _const_spec(shape):
    nd = len(shape)
    return pl.BlockSpec(shape, lambda *_: (0,) * nd, pipeline_mode=pl.Buffered(1))


def _layer_spec(arr, layer):
    nd = arr.ndim - 1
    return pl.BlockSpec((None,) + arr.shape[1:], lambda *_: (layer,) + (0,) * nd,
                        pipeline_mode=pl.Buffered(1))


def _params(*sem):
    return pltpu.CompilerParams(dimension_semantics=sem, vmem_limit_bytes=VMEM_LIMIT)


def _rms(x, g):
    return x * lax.rsqrt(jnp.mean(x * x, axis=-1, keepdims=True) + RMS_EPS) * g


def _dot(a, b):
    return jnp.dot(a, b, preferred_element_type=F32)


def _static_loop(n, body):
    for j in range(n):
        body(j)


def _residual(refs, rows=slice(None)):
    if len(refs) == 1:
        return refs[0][rows, :]
    h_ref, ya_ref, yb_ref, route_ref = refs
    route = route_ref[rows, :]
    ga = route[:, GATE_LANE:GATE_LANE + 1]
    gb = route[:, GATE_LANE + 1:GATE_LANE + 2]
    return h_ref[rows, :] + ga * ya_ref[rows, :] + gb * yb_ref[rows, :]


def _residual_specs(h_parts, tm, index_map):
    return [pl.BlockSpec((tm, p.shape[1]), index_map) for p in h_parts]


def _even_mixer_kernel(*refs, tm, cdim, nres):
    res_refs = refs[:nres]
    g_ref, win_ref, cw_ref, pw_ref, ps_ref, wout_ref, o_ref, zc_ext, p_ext = refs[nres:]
    j = pl.program_id(1)

    @pl.when(j == 0)
    def _():
        zc_ext[0:HALO, :] = jnp.zeros((HALO, cdim), F32)
        p_ext[0:HALO, :] = jnp.zeros((HALO, cdim), F32)

    rb = tm // EVEN_ROW_BLOCKS
    for q in range(EVEN_ROW_BLOCKS):
        r0 = q * rb
        e0 = HALO + r0
        h = _residual(res_refs, slice(r0, r0 + rb))
        hn = _rms(h, g_ref[...]).astype(BF16)
        z = _dot(hn, win_ref[...])
        gb = z[:, 0:cdim]
        zc = z[:, cdim:2 * cdim] * z[:, 2 * cdim:3 * cdim]
        p = z[:, 3 * cdim:4 * cdim]
        zc_ext[e0:e0 + rb, :] = zc
        p_ext[e0:e0 + rb, :] = p

        y = cw_ref[CONV_WIDTH - 1:CONV_WIDTH, :] * zc
        for k in range(CONV_WIDTH - 1):
            sh = CONV_WIDTH - 1 - k
            y = y + cw_ref[k:k + 1, :] * zc_ext[e0 - sh:e0 - sh + rb, :]
        parts = [(gb * y).astype(BF16)]

        pos = (j * tm + r0 + 1 + lax.broadcasted_iota(jnp.int32, (rb, LANES), 0)).astype(F32)
        for gi, w in enumerate(POOL_WINDOWS):
            c0 = gi * LANES
            pg = p[:, c0:c0 + LANES]
            s = pg
            for k in range(1, w):
                s = s + p_ext[e0 - k:e0 - k + rb, c0:c0 + LANES]
            pooled = s / jnp.minimum(pos, float(w)) - pg
            mixed = _dot(pooled.astype(BF16), pw_ref[gi]) * ps_ref[:, c0:c0 + LANES]
            parts.append(mixed.astype(BF16))

        mix = jnp.concatenate(parts, axis=1)
        o_ref[r0:r0 + rb, :] = h + _dot(mix, wout_ref[...])

    zc_ext[0:HALO, :] = zc_ext[tm:tm + HALO, :]
    p_ext[0:HALO, :] = p_ext[tm:tm + HALO, :]


def _even_mixer(h_parts, g, w_in, conv_w, pool_w, pool_scale, w_out, *, layer, bsz, seq):
    n, d = h_parts[0].shape
    cdim = w_in.shape[2] // 4
    tm = min(TOKEN_TILE, seq)
    nj = seq // tm
    kern = functools.partial(_even_mixer_kernel, tm=tm, cdim=cdim, nres=len(h_parts))
    return pl.pallas_call(
        kern,
        out_shape=jax.ShapeDtypeStruct((n, d), F32),
        grid=(bsz, nj),
        in_specs=_residual_specs(h_parts, tm, lambda b, j: (b * nj + j, 0)) + [
            _layer_spec(g, layer),
            _layer_spec(w_in, layer),
            _layer_spec(conv_w, layer),
            _layer_spec(pool_w, layer),
            _layer_spec(pool_scale, layer),
            _layer_spec(w_out, layer),
        ],
        out_specs=pl.BlockSpec((tm, d), lambda b, j: (b * nj + j, 0)),
        scratch_shapes=[pltpu.VMEM((HALO + tm, cdim), F32), pltpu.VMEM((HALO + tm, cdim), F32)],
        compiler_params=_params("arbitrary", "arbitrary"),
        name="even_mixer",
    )(*h_parts, g, w_in, conv_w, pool_w, pool_scale, w_out)


def _ffn_kernel(h_ref, g_ref, w1_ref, w3_ref, w2_ref, o_ref, *, fdim):
    h = h_ref[...]
    hn = _rms(h, g_ref[...]).astype(BF16)
    acc = h
    for f0 in range(0, fdim, FFN_F_TILE):
        f1 = min(f0 + FFN_F_TILE, fdim)
        a = _dot(hn, w1_ref[:, f0:f1])
        b = _dot(hn, w3_ref[:, f0:f1])
        act = (a * jax.nn.sigmoid(a) * b).astype(BF16)
        acc = acc + _dot(act, w2_ref[f0:f1, :])
    o_ref[...] = acc


def _ffn(h, g, w1, w3, w2, *, layer):
    n, d = h.shape
    fdim = w1.shape[2]
    tm = min(TOKEN_TILE, n)
    return pl.pallas_call(
        functools.partial(_ffn_kernel, fdim=fdim),
        out_shape=jax.ShapeDtypeStruct((n, d), F32),
        grid=(n // tm,),
        in_specs=[
            pl.BlockSpec((tm, d), lambda i: (i, 0)),
            _layer_spec(g, layer),
            _layer_spec(w1, layer),
            _layer_spec(w3, layer),
            _layer_spec(w2, layer),
        ],
        out_specs=pl.BlockSpec((tm, d), lambda i: (i, 0)),
        compiler_params=_params("arbitrary"),
        name="dense_swiglu",
    )(h, g, w1, w3, w2)


def _odd_in_kernel(h_ref, g_ref, wt_ref, o_ref, *, nck):
    hn = _rms(h_ref[...], g_ref[...]).astype(BF16)
    zt = lax.dot_general(wt_ref[...], hn, (((1,), (1,)), ((), ())), preferred_element_type=F32)
    for c in range(nck):
        o_ref[c] = zt[:, c * LANES:(c + 1) * LANES]


def _odd_in(h, g, w_in_t, *, layer):
    n, d = h.shape
    cols = w_in_t.shape[1]
    tm = min(LIGHT_TOKEN_TILE, n)
    nck = tm // LANES
    return pl.pallas_call(
        functools.partial(_odd_in_kernel, nck=nck),
        out_shape=jax.ShapeDtypeStruct((n // LANES, cols, LANES), F32),
        grid=(n // tm,),
        in_specs=[
            pl.BlockSpec((tm, d), lambda i: (i, 0)),
            _layer_spec(g, layer),
            _layer_spec(w_in_t, layer),
        ],
        out_specs=pl.BlockSpec((nck, cols, LANES), lambda i: (i, 0, 0)),
        compiler_params=_params("arbitrary"),
        name="odd_in_proj",
    )(h, g, w_in_t)


def _s5_kernel(u_ref, kt_ref, ws_ref, wc_ref, coef_ref, y_ref, m_scr, *, hdim, cpb, n_steps):
    t = T_SSM
    row = lax.broadcasted_iota(jnp.int32, (t, t), 0)
    col = lax.broadcasted_iota(jnp.int32, (t, t), 1)
    causal = col >= row
    def expand_row_block(j):
        r0 = j * t if isinstance(j, int) else pl.multiple_of(j * t, t)
        for i in range(hdim):
            kb = jnp.broadcast_to(kt_ref[0, j, i:i + 1, :], (t, t))
            tz = pltpu.roll(kb, 0, 1, stride=1, stride_axis=0)
            m_scr[pl.ds(r0, t), i * t:(i + 1) * t] = jnp.where(causal, tz, 0.0).astype(BF16)

    _static_loop(hdim, expand_row_block)

    x = jnp.concatenate([u_ref[:, c, :] for c in range(hdim)], axis=1).astype(BF16)
    y = _dot(x, m_scr[...])
    st = _dot(x, ws_ref[0])
    half = st.shape[1] // 2
    rb = lax.broadcasted_iota(jnp.int32, st.shape, 0) & (cpb - 1)
    for k in range(n_steps):
        d = 1 << k
        sh = jnp.where(rb >= d, pltpu.roll(st, d, 0), 0.0)
        sw = pltpu.roll(sh, half, 1)
        st = st + coef_ref[0, k, 0:1, :] * sh + coef_ref[0, k, 1:2, :] * sw
    prev = jnp.where(rb >= 1, pltpu.roll(st, 1, 0), 0.0)
    wc = jnp.concatenate([wc_ref[0, i] for i in range(hdim)], axis=1)
    y = y + _dot(prev.astype(BF16), wc)
    for i in range(hdim):
        y_ref[:, i, :] = y[:, i * t:(i + 1) * t]


def _s5(zt3, kt, ws, wc, coef, *, layer, groups, cpb):
    nc = zt3.shape[0]
    g, hdim = groups, kt.shape[1]
    n_steps = coef.shape[1]
    g0 = layer * groups
    return pl.pallas_call(
        functools.partial(_s5_kernel, hdim=hdim, cpb=cpb, n_steps=n_steps),
        out_shape=jax.ShapeDtypeStruct((nc, g * hdim, T_SSM), F32),
        grid=(g,),
        in_specs=[
            pl.BlockSpec((nc, hdim, T_SSM), lambda i: (0, i, 0)),
            pl.BlockSpec((1,) + kt.shape[1:], lambda i: (g0 + i, 0, 0, 0)),
            pl.BlockSpec((1,) + ws.shape[1:], lambda i: (g0 + i, 0, 0)),
            pl.BlockSpec((1,) + wc.shape[1:], lambda i: (g0 + i, 0, 0, 0)),
            pl.BlockSpec((1,) + coef.shape[1:], lambda i: (g0 + i, 0, 0, 0)),
        ],
        out_specs=pl.BlockSpec((nc, hdim, T_SSM), lambda i: (0, i, 0)),
        scratch_shapes=[pltpu.VMEM((hdim * T_SSM, hdim * T_SSM), BF16)],
        compiler_params=_params("arbitrary"),
        name="s5_chunked",
    )(zt3, kt, ws, wc, coef)


def _s5_tables(lam_re, lam_im, log_dt, b_re, b_im, c_re, c_im, *, cpb):
    flat = lambda v: v.reshape((-1,) + v.shape[2:])
    cat = lambda u, v: jnp.concatenate([u, v], axis=-1)
    lr, li, log_dt = flat(lam_re), flat(lam_im), flat(log_dt)
    b_re, b_im, c_re, c_im = flat(b_re), flat(b_im), flat(c_re), flat(c_im)
    gg, p = lr.shape
    hdim = b_re.shape[2]
    t = T_SSM
    dt = jnp.exp(log_dt)[:, None]
    mag = jnp.exp(lr * dt)
    abar_r = mag * jnp.cos(li * dt)
    abar_i = mag * jnp.sin(li * dt)
    qr, qi = abar_r - 1.0, abar_i
    den = lr * lr + li * li
    fr = ((qr * lr + qi * li) / den)[..., None]
    fi = ((qi * lr - qr * li) / den)[..., None]
    bjr = (fr * b_re - fi * b_im).transpose(0, 2, 1)
    bji = (fr * b_im + fi * b_re).transpose(0, 2, 1)

    def apow(k):
        kk = k[None, :, None]
        m = jnp.exp(kk * (lr * dt)[:, None, :])
        th = kk * (li * dt)[:, None, :]
        return m * jnp.cos(th), m * jnp.sin(th)

    par, pai = apow(jnp.arange(t + 1, dtype=F32))
    pr, pi = par[:, :t], pai[:, :t]
    p1r, p1i = par[:, 1:], pai[:, 1:]
    cbr = c_re[:, None] * bjr[:, :, None, :] - c_im[:, None] * bji[:, :, None, :]
    cbi = c_re[:, None] * bji[:, :, None, :] + c_im[:, None] * bjr[:, :, None, :]
    kt = jnp.einsum('gjic,glc->gjil', cat(cbr, -cbi), cat(pr, pi),
                    precision=lax.Precision.HIGHEST)
    er, ei = apow(jnp.arange(t - 1, -1, -1, dtype=F32))
    ws = (cat(er, er)[:, None] * cat(bjr, bji)[:, :, None, :]
          + cat(-ei, ei)[:, None] * cat(bji, bjr)[:, :, None, :])
    ws = ws.astype(BF16).reshape(gg, hdim * t, 2 * p)
    p1rt = p1r.transpose(0, 2, 1)
    p1it = p1i.transpose(0, 2, 1)
    wc = (cat(c_re, -c_im)[..., None] * jnp.concatenate([p1rt, p1rt], axis=1)[:, None]
          + cat(-c_im, -c_re)[..., None] * jnp.concatenate([p1it, p1it], axis=1)[:, None])
    wc = wc.astype(BF16)
    n_steps = max(1, int(math.log2(cpb)))
    sr, si = apow(float(t) * (2.0 ** jnp.arange(n_steps, dtype=F32)))
    coef = jnp.stack([cat(sr, sr), cat(-si, si)], axis=2)
    return kt, ws, wc, coef


def _odd_tail_kernel(z_ref, y_ref, h_ref, d_ref, gw_ref, gb_ref, lng_ref, sw_ref, sb_ref,
                     wout_ref, g2_ref, rwh_ref, rwl_ref, o_ref, xn_ref, route_ref,
                     *, nck, sdim, heads, n_experts):
    def cat(ref, r0, r1):
        return jnp.concatenate([ref[c, r0:r1, :] for c in range(nck)], axis=1)

    def rep(ref, r0=None, r1=None):
        v = ref[...] if r0 is None else ref[r0:r1, :]
        return jnp.concatenate([v] * nck, axis=1)

    ut = cat(z_ref, 0, sdim)
    yt = cat(y_ref, 0, sdim)
    gt = jax.nn.gelu(yt + rep(d_ref) * ut)
    glu = _dot(gw_ref[...], gt.astype(BF16)) + rep(gb_ref)
    parts = [gt * jax.nn.sigmoid(glu)]

    hd = sdim // heads
    for hh in range(heads):
        r0 = sdim + hh * hd
        up = jax.nn.gelu(cat(z_ref, r0, r0 + hd))
        vp = jax.nn.gelu(cat(z_ref, sdim + r0, sdim + r0 + hd))
        mu = jnp.mean(vp, axis=0, keepdims=True)
        vc = vp - mu
        var = jnp.mean(vc * vc, axis=0, keepdims=True)
        vn = vc * lax.rsqrt(var + LN_EPS) * rep(lng_ref, hh * hd, (hh + 1) * hd)
        stacked = jnp.concatenate([vn[:, c * LANES:(c + 1) * LANES] for c in range(nck)], axis=0)
        s = _dot(stacked.astype(BF16), sw_ref[hh]) + sb_ref[hh]
        s = jnp.concatenate([s[c * hd:(c + 1) * hd, :] for c in range(nck)], axis=1)
        parts.append(up * s)

    mix = jnp.concatenate(parts, axis=0).T.astype(BF16)
    h1 = h_ref[...] + _dot(mix, wout_ref[...])
    o_ref[...] = h1

    xn = _rms(h1, g2_ref[...])
    xn_ref[...] = xn
    xh = xn.astype(BF16)
    xl = (xn - xh.astype(F32)).astype(BF16)
    logits = _dot(xh, rwh_ref[...]) + (_dot(xh, rwl_ref[...]) + _dot(xl, rwh_ref[...]))
    lane = lax.broadcasted_iota(jnp.int32, logits.shape, 1)
    neg = jnp.float32(-jnp.inf)
    lg = jnp.where(lane < n_experts, logits, neg)
    m1 = jnp.max(lg, axis=1, keepdims=True)
    i1 = jnp.min(jnp.where(lg == m1, lane, LANES), axis=1, keepdims=True)
    lg2 = jnp.where(lane == i1, neg, lg)
    m2 = jnp.max(lg2, axis=1, keepdims=True)
    i2 = jnp.min(jnp.where(lg2 == m2, lane, LANES), axis=1, keepdims=True)
    e2 = jnp.exp(m2 - m1)
    w1 = 1.0 / (1.0 + e2)
    w2 = e2 / (1.0 + e2)
    route_ref[...] = (jnp.where(lane == ID_LANE, i1.astype(F32), 0.0)
                      + jnp.where(lane == ID_LANE + 1, i2.astype(F32), 0.0)
                      + jnp.where(lane == GATE_LANE, w1, 0.0)
                      + jnp.where(lane == GATE_LANE + 1, w2, 0.0))


def _odd_tail(zt3, yt3, h, d_b, glu_wt, glu_b_b, lng_b, sgu_wt, sgu_b, w_out, g2, router_hi,
              router_lo,
              *, layer, heads, n_experts):
    n, dm = h.shape
    cols = zt3.shape[1]
    sdim = yt3.shape[1]
    tm = min(TOKEN_TILE, n)
    nck = tm // LANES
    kern = functools.partial(_odd_tail_kernel, nck=nck, sdim=sdim, heads=heads, n_experts=n_experts)
    return pl.pallas_call(
        kern,
        out_shape=(jax.ShapeDtypeStruct((n, dm), F32),
                   jax.ShapeDtypeStruct((n, dm), F32),
                   jax.ShapeDtypeStruct((n, LANES), F32)),
        grid=(n // tm,),
        in_specs=[
            pl.BlockSpec((nck, cols, LANES), lambda i: (i, 0, 0)),
            pl.BlockSpec((nck, sdim, LANES), lambda i: (i, 0, 0)),
            pl.BlockSpec((tm, dm), lambda i: (i, 0)),
            _const_spec(d_b.shape),
            _layer_spec(glu_wt, layer),
            _const_spec(glu_b_b.shape),
            _const_spec(lng_b.shape),
            _const_spec(sgu_wt.shape),
            _const_spec(sgu_b.shape),
            _layer_spec(w_out, layer),
            _layer_spec(g2, layer),
            _const_spec(router_hi.shape),
            _const_spec(router_lo.shape),
        ],
        out_specs=(pl.BlockSpec((tm, dm), lambda i: (i, 0)),
                   pl.BlockSpec((tm, dm), lambda i: (i, 0)),
                   pl.BlockSpec((tm, LANES), lambda i: (i, 0))),
        compiler_params=_params("arbitrary"),
        name="odd_mixer_tail",
    )(zt3, yt3, h, d_b, glu_wt, glu_b_b, lng_b, sgu_wt, sgu_b, w_out, g2, router_hi, router_lo)


def _experts_kernel(te_ref, ns_ref, na_ref, x_ref, w1_ref, w3_ref, w2_ref, o_ref, *, sub):
    f = pl.program_id(1)

    @pl.when(f == 0)
    def _():
        o_ref[...] = jnp.zeros(o_ref.shape, F32)

    def swiglu_rows(start, nrows):
        rows = pl.ds(pl.multiple_of(start, sub), nrows)
        x = x_ref[rows, :].astype(BF16)
        a = _dot(x, w1_ref[0].astype(BF16))
        b = _dot(x, w3_ref[0].astype(BF16))
        act = (a * jax.nn.sigmoid(a) * b).astype(BF16)
        o_ref[rows, :] += _dot(act, w2_ref[0].astype(BF16))

    nsub = ns_ref[pl.program_id(0)]
    npair = nsub // 2

    def pair(s, carry):
        swiglu_rows(s * (2 * sub), 2 * sub)
        return carry

    lax.fori_loop(0, npair, pair, 0)

    @pl.when(nsub % 2 == 1)
    def _():
        swiglu_rows(npair * (2 * sub), sub)


def _experts(tile_expert, tile_nsub, n_active, xs, w1, w3, w2, *, layer):
    r, d = xs.shape
    fdim = w1.shape[3]
    te = EXPERT_TILE
    tf = min(EXPERT_F_TILE, fdim)
    n_tiles = r // te
    nf = fdim // tf

    def row_map(i, f, te_ref, ns_ref, na_ref):
        return (jnp.minimum(i, na_ref[0] - 1), 0)

    grid_spec = pltpu.PrefetchScalarGridSpec(
        num_scalar_prefetch=3,
        grid=(n_tiles, nf),
        in_specs=[
            pl.BlockSpec((te, d), row_map),
            pl.BlockSpec((None, 1, d, tf), lambda i, f, te_ref, *_: (layer, te_ref[i], 0, f)),
            pl.BlockSpec((None, 1, d, tf), lambda i, f, te_ref, *_: (layer, te_ref[i], 0, f)),
            pl.BlockSpec((None, 1, tf, d), lambda i, f, te_ref, *_: (layer, te_ref[i], f, 0)),
        ],
        out_specs=pl.BlockSpec((te, d), lambda i, f, *_: (i, 0)),
    )
    return pl.pallas_call(
        functools.partial(_experts_kernel, sub=min(EXPERT_SUB, te)),
        out_shape=jax.ShapeDtypeStruct((r, d), F32),
        grid_spec=grid_spec,
        compiler_params=_params("arbitrary", "arbitrary"),
        name="grouped_experts",
    )(tile_expert, tile_nsub, n_active, xs, w1, w3, w2)


def _dispatch_kernel(te_ref, base_ref, quota_ref, cnt_ref, cb_ref, ct_ref, o_ref, *, te, nb, n):
    i = pl.program_id(0)
    e = te_ref[i]
    lane = lax.broadcasted_iota(jnp.int32, (1, te), 1)
    row = i * te + lane
    rank = base_ref[i] + lane
    blk_end = cb_ref[0][:, LANES - 1:LANES]
    blk = jnp.sum((blk_end <= rank).astype(jnp.int32), axis=0, keepdims=True)
    blk = jnp.minimum(blk, nb - 1)
    onehot = (lax.broadcasted_iota(jnp.int32, (nb, te), 0) == blk).astype(F32)
    counts_t = jnp.dot(ct_ref[0].astype(F32), onehot, precision=lax.Precision.HIGHEST,
                       preferred_element_type=F32)
    within = jnp.sum((counts_t <= rank.astype(F32)).astype(jnp.int32), axis=0, keepdims=True)
    filler = row & (n - 1) if n & (n - 1) == 0 else lax.rem(row, n)
    routed = (lane < quota_ref[i]) & (rank < cnt_ref[e])
    o_ref[0] = jnp.where(routed, blk * LANES + within, filler)


def _dispatch_rows(tile_expert, tile_base, tile_quota, counts, cb, ct, *, te, n):
    n_exp, nb, _ = cb.shape
    n_tiles = tile_expert.shape[0]
    grid_spec = pltpu.PrefetchScalarGridSpec(
        num_scalar_prefetch=4,
        grid=(n_tiles,),
        in_specs=[
            pl.BlockSpec((1, nb, LANES), lambda i, te_ref, *_: (te_ref[i], 0, 0)),
            pl.BlockSpec((1, LANES, nb), lambda i, te_ref, *_: (te_ref[i], 0, 0)),
        ],
        out_specs=pl.BlockSpec((1, 1, te), lambda i, *_: (i, 0, 0)),
    )
    return pl.pallas_call(
        functools.partial(_dispatch_kernel, te=te, nb=nb, n=n),
        out_shape=jax.ShapeDtypeStruct((n_tiles, 1, te), jnp.int32),
        grid_spec=grid_spec,
        compiler_params=_params("arbitrary"),
        name="dispatch_rows",
    )(tile_expert, tile_base, tile_quota, counts, cb, ct)


def _moe(h1, xn, route, w1, w3, w2, *, layer):
    n, d = h1.shape
    n_exp = w1.shape[1]
    te = EXPERT_TILE
    sub = min(EXPERT_SUB, te)
    n_tiles = (n * TOP_K + te - 1) // te + n_exp
    nb = n // LANES

    ids = route[:, ID_LANE:ID_LANE + TOP_K].astype(jnp.int32)
    chosen = ids[:, :, None] == jnp.arange(n_exp, dtype=jnp.int32)[None, None, :]
    sel = chosen.any(axis=1)
    csum = jnp.cumsum(sel.astype(jnp.int32), axis=0)
    counts = csum[-1]
    tiles_e = (counts + te - 1) // te
    quota = (counts + jnp.maximum(tiles_e, 1) - 1) // jnp.maximum(tiles_e, 1)
    quota = ((quota + sub - 1) // sub) * sub
    padded = tiles_e * te
    ends = jnp.cumsum(padded)
    offs = ends - padded
    tile_start = jnp.arange(n_tiles, dtype=jnp.int32) * te
    in_use = tile_start < ends[-1]
    n_active = (ends[-1] // te).astype(jnp.int32)
    tile_expert = jnp.sum(tile_start[:, None] >= ends[None, :], axis=1).astype(jnp.int32)
    last_expert = jnp.take(tile_expert, jnp.maximum(n_active - 1, 0))
    tile_expert = jnp.where(in_use, tile_expert, last_expert)
    tile_quota = jnp.where(in_use, quota[tile_expert], 0).astype(jnp.int32)
    tile_base = ((tile_start - offs[tile_expert]) // te * tile_quota).astype(jnp.int32)
    valid = jnp.clip(counts[tile_expert] - tile_base, 0, tile_quota)
    tile_nsub = ((valid + sub - 1) // sub).astype(jnp.int32)

    cb = csum.T.reshape(n_exp, nb, LANES)
    src = _dispatch_rows(tile_expert, tile_base, tile_quota, counts.astype(jnp.int32), cb,
                         cb.transpose(0, 2, 1), te=te, n=n)
    xs = jnp.take(xn, src.reshape(-1), axis=0, mode='clip')
    ys = _experts(tile_expert, tile_nsub, n_active.reshape(1), xs, w1, w3, w2, layer=layer)
    rank = csum - 1
    tile_k = rank // jnp.maximum(quota, 1)[None, :]
    dest = offs[None, :] + tile_k * te + (rank - tile_k * quota[None, :])
    pos = jnp.sum(jnp.where(chosen, dest[:, None, :], 0), axis=2)
    return (h1, jnp.take(ys, pos[:, 0], axis=0, mode='clip'),
            jnp.take(ys, pos[:, 1], axis=0, mode='clip'), route)


def _final_norm_kernel(*refs):
    *res_refs, g_ref, o_ref = refs
    o_ref[...] = _rms(_residual(res_refs), g_ref[...])


def _final_norm(h_parts, g):
    n, d = h_parts[0].shape
    tm = min(LIGHT_TOKEN_TILE, n)
    return pl.pallas_call(
        _final_norm_kernel,
        out_shape=jax.ShapeDtypeStruct((n, d), F32),
        grid=(n // tm,),
        in_specs=_residual_specs(h_parts, tm, lambda i: (i, 0)) + [_const_spec((1, d))],
        out_specs=pl.BlockSpec((tm, d), lambda i: (i, 0)),
        compiler_params=_params("arbitrary"),
        name="final_norm",
    )(*h_parts, g)


def _even_layer(h_parts, p, *, bsz, seq):
    h = _even_mixer(h_parts, p['ev_norm1_g'], p['ev_w_in'], p['ev_conv_w'], p['ev_pool_w'],
                    p['ev_pool_scale'], p['ev_w_out'], layer=0, bsz=bsz, seq=seq)
    return _ffn(h, p['ev_norm2_g'], p['ev_ffn_w1'], p['ev_ffn_w3'], p['ev_ffn_w2'], layer=0)


def _odd_layer(h, p, moe_w, *, moe_layer, seq):
    sgu_w = p['od_sgu_w'][0]
    router_w = p['od_router_w'][0]
    heads, slen, _ = sgu_w.shape
    n_exp = router_w.shape[1]
    assert slen == LANES and seq % T_SSM == 0
    cpb = seq // T_SSM
    assert cpb & (cpb - 1) == 0, "chunks per sequence must be a power of two"

    zt3 = _odd_in(h, p['od_norm1_g'], p['od_w_in_t'], layer=0)
    tables = _s5_tables(p['od_lambda_re'], p['od_lambda_im'], p['od_log_dt'], p['od_b_re'],
                        p['od_b_im'], p['od_c_re'], p['od_c_im'], cpb=cpb)
    yt3 = _s5(zt3, *tables, layer=0, groups=p['od_lambda_re'].shape[1], cpb=cpb)

    lanes_b = lambda v: jnp.broadcast_to(v[:, None], (v.shape[0], LANES))
    cidx = jnp.arange(slen) // CHUNK
    mask = cidx[None, :] <= cidx[:, None]
    sgu_wt = jnp.where(mask[None], sgu_w, 0.0).transpose(0, 2, 1).astype(BF16)
    router_pad = jnp.zeros((router_w.shape[0], LANES), F32).at[:, :n_exp].set(router_w)
    router_hi = router_pad.astype(BF16)
    router_lo = (router_pad - router_hi.astype(F32)).astype(BF16)
    h1, xn, route = _odd_tail(
        zt3, yt3, h, lanes_b(p['od_d'][0]), p['od_glu_wt'], lanes_b(p['od_glu_b'][0]),
        lanes_b(p['od_sgu_ln_g'][0]), sgu_wt, p['od_sgu_b'][0][:, None, :], p['od_w_out'],
        p['od_norm2_g'], router_hi, router_lo, layer=0, heads=heads, n_experts=n_exp)
    return _moe(h1, xn, route, *moe_w, layer=moe_layer)


def kernel(x, ev_norm1_g, ev_w_in, ev_conv_w, ev_pool_w, ev_pool_scale, ev_w_out, ev_norm2_g, ev_ffn_w1, ev_ffn_w3, ev_ffn_w2, od_norm1_g, od_w_in, od_lambda_re, od_lambda_im, od_log_dt, od_b_re, od_b_im, od_c_re, od_c_im, od_d, od_glu_w, od_glu_b, od_sgu_ln_g, od_sgu_w, od_sgu_b, od_w_out, od_norm2_g, od_router_w, od_moe_w1, od_moe_w3, od_moe_w2, final_norm_g):
    bsz, seq, d = x.shape
    depth = ev_norm1_g.shape[0] + od_norm1_g.shape[0]
    row = lambda v: v[:, None, :]

    def even_params(i):
        one = lambda v: v[i:i + 1]
        return dict(
            ev_norm1_g=row(one(ev_norm1_g)), ev_w_in=one(ev_w_in).astype(BF16),
            ev_conv_w=one(ev_conv_w), ev_pool_w=one(ev_pool_w).astype(BF16),
            ev_pool_scale=row(one(ev_pool_scale)), ev_w_out=one(ev_w_out).astype(BF16),
            ev_norm2_g=row(one(ev_norm2_g)), ev_ffn_w1=one(ev_ffn_w1).astype(BF16),
            ev_ffn_w3=one(ev_ffn_w3).astype(BF16), ev_ffn_w2=one(ev_ffn_w2).astype(BF16))

    def odd_params(i):
        one = lambda v: v[i:i + 1]
        return dict(
            od_norm1_g=row(one(od_norm1_g)), od_w_in_t=one(od_w_in).transpose(0, 2, 1).astype(BF16),
            od_lambda_re=one(od_lambda_re), od_lambda_im=one(od_lambda_im),
            od_log_dt=one(od_log_dt), od_b_re=one(od_b_re), od_b_im=one(od_b_im),
            od_c_re=one(od_c_re), od_c_im=one(od_c_im), od_d=one(od_d),
            od_glu_wt=one(od_glu_w).transpose(0, 2, 1).astype(BF16), od_glu_b=one(od_glu_b),
            od_sgu_ln_g=one(od_sgu_ln_g), od_sgu_w=one(od_sgu_w), od_sgu_b=one(od_sgu_b),
            od_w_out=one(od_w_out).astype(BF16), od_norm2_g=row(one(od_norm2_g)),
            od_router_w=one(od_router_w))

    moe_w = (od_moe_w1, od_moe_w3, od_moe_w2)
    h_parts = (x.reshape(bsz * seq, d),)
    for layer in range(depth):
        if layer % 2 == 0:
            h_parts = (_even_layer(h_parts, even_params(layer // 2), bsz=bsz, seq=seq),)
        else:
            assert len(h_parts) == 1
            h_parts = _odd_layer(h_parts[0], odd_params(layer // 2), moe_w,
                                 moe_layer=layer // 2, seq=seq)
    return _final_norm(h_parts, final_norm_g[None]).reshape(bsz, seq, d)
```

```python
import functools
import math

import jax
import jax.numpy as jnp
from jax import lax
from jax.experimental import pallas as pl
from jax.experimental.pallas import tpu as pltpu

F32 = jnp.float32
BF16 = jnp.bfloat16

RMS_EPS = 1e-5
LN_EPS = 1e-5
CHUNK = 64
POOL_WINDOWS = (2, 4, 8, 16)
CONV_WIDTH = 3
TOP_K = 2

LANES = 128
V7X_VMEM_BYTES = 64 * 1024 * 1024
VMEM_LIMIT = V7X_VMEM_BYTES * 7 // 8
HALO = 16
T_SSM = LANES
TOKEN_TILE = 512
LIGHT_TOKEN_TILE = 1024
EVEN_ROW_BLOCKS = 2
ID_LANE = LANES - 4
GATE_LANE = LANES - 2
EXPERT_TILE = 2048
EXPERT_SUB = 512
EXPERT_F_TILE = 512
FFN_F_TILE = 1024


def _const_spec(shape):
    nd = len(shape)
    return pl.BlockSpec(shape, lambda *_: (0,) * nd, pipeline_mode=pl.Buffered(1))


def _layer_spec(arr, layer):
    nd = arr.ndim - 1
    return pl.BlockSpec((None,) + arr.shape[1:], lambda *_: (layer,) + (0,) * nd,
                        pipeline_mode=pl.Buffered(1))


def _params(*sem):
    return pltpu.CompilerParams(dimension_semantics=sem, vmem_limit_bytes=VMEM_LIMIT)


def _rms(x, g):
    return x * lax.rsqrt(jnp.mean(x * x, axis=-1, keepdims=True) + RMS_EPS) * g


def _dot(a, b):
    return jnp.dot(a, b, preferred_element_type=F32)


def _static_loop(n, body):
    for j in range(n):
        body(j)


def _residual(refs, rows=slice(None)):
    if len(refs) == 1:
        return refs[0][rows, :]
    h_ref, ya_ref, yb_ref, route_ref = refs
    route = route_ref[rows, :]
    ga = route[:, GATE_LANE:GATE_LANE + 1]
    gb = route[:, GATE_LANE + 1:GATE_LANE + 2]
    return h_ref[rows, :] + ga * ya_ref[rows, :] + gb * yb_ref[rows, :]


def _residual_specs(h_parts, tm, index_map):
    return [pl.BlockSpec((tm, p.shape[1]), index_map) for p in h_parts]


def _even_mixer_kernel(*refs, tm, cdim, nres):
    res_refs = refs[:nres]
    g_ref, win_ref, cw_ref, pw_ref, ps_ref, wout_ref, o_ref, zc_ext, p_ext = refs[nres:]
    j = pl.program_id(1)

    @pl.when(j == 0)
    def _():
        zc_ext[0:HALO, :] = jnp.zeros((HALO, cdim), F32)
        p_ext[0:HALO, :] = jnp.zeros((HALO, cdim), F32)

    rb = tm // EVEN_ROW_BLOCKS
    for q in range(EVEN_ROW_BLOCKS):
        r0 = q * rb
        e0 = HALO + r0
        h = _residual(res_refs, slice(r0, r0 + rb))
        hn = _rms(h, g_ref[...]).astype(BF16)
        z = _dot(hn, win_ref[...])
        gb = z[:, 0:cdim]
        zc = z[:, cdim:2 * cdim] * z[:, 2 * cdim:3 * cdim]
        p = z[:, 3 * cdim:4 * cdim]
        zc_ext[e0:e0 + rb, :] = zc
        p_ext[e0:e0 + rb, :] = p

        y = cw_ref[CONV_WIDTH - 1:CONV_WIDTH, :] * zc
        for k in range(CONV_WIDTH - 1):
            sh = CONV_WIDTH - 1 - k
            y = y + cw_ref[k:k + 1, :] * zc_ext[e0 - sh:e0 - sh + rb, :]
        parts = [(gb * y).astype(BF16)]

        pos = (j * tm + r0 + 1 + lax.broadcasted_iota(jnp.int32, (rb, LANES), 0)).astype(F32)
        for gi, w in enumerate(POOL_WINDOWS):
            c0 = gi * LANES
            pg = p[:, c0:c0 + LANES]
            s = pg
            for k in range(1, w):
                s = s + p_ext[e0 - k:e0 - k + rb, c0:c0 + LANES]
            pooled = s / jnp.minimum(pos, float(w)) - pg
            mixed = _dot(pooled.astype(BF16), pw_ref[gi]) * ps_ref[:, c0:c0 + LANES]
            parts.append(mixed.astype(BF16))

        mix = jnp.concatenate(parts, axis=1)
        o_ref[r0:r0 + rb, :] = h + _dot(mix, wout_ref[...])

    zc_ext[0:HALO, :] = zc_ext[tm:tm + HALO, :]
    p_ext[0:HALO, :] = p_ext[tm:tm + HALO, :]


def _even_mixer(h_parts, g, w_in, conv_w, pool_w, pool_scale, w_out, *, layer, bsz, seq):
    n, d = h_parts[0].shape
    cdim = w_in.shape[2] // 4
    tm = min(TOKEN_TILE, seq)
    nj = seq // tm
    kern = functools.partial(_even_mixer_kernel, tm=tm, cdim=cdim, nres=len(h_parts))
    return pl.pallas_call(
        kern,
        out_shape=jax.ShapeDtypeStruct((n, d), F32),
        grid=(bsz, nj),
        in_specs=_residual_specs(h_parts, tm, lambda b, j: (b * nj + j, 0)) + [
            _layer_spec(g, layer),
            _layer_spec(w_in, layer),
            _layer_spec(conv_w, layer),
            _layer_spec(pool_w, layer),
            _layer_spec(pool_scale, layer),
            _layer_spec(w_out, layer),
        ],
        out_specs=pl.BlockSpec((tm, d), lambda b, j: (b * nj + j, 0)),
        scratch_shapes=[pltpu.VMEM((HALO + tm, cdim), F32), pltpu.VMEM((HALO + tm, cdim), F32)],
        compiler_params=_params("arbitrary", "arbitrary"),
        name="even_mixer",
    )(*h_parts, g, w_in, conv_w, pool_w, pool_scale, w_out)


def _ffn_kernel(h_ref, g_ref, w1_ref, w3_ref, w2_ref, o_ref, *, fdim):
    h = h_ref[...]
    hn = _rms(h, g_ref[...]).astype(BF16)
    acc = h
    for f0 in range(0, fdim, FFN_F_TILE):
        f1 = min(f0 + FFN_F_TILE, fdim)
        a = _dot(hn, w1_ref[:, f0:f1])
        b = _dot(hn, w3_ref[:, f0:f1])
        act = (a * jax.nn.sigmoid(a) * b).astype(BF16)
        acc = acc + _dot(act, w2_ref[f0:f1, :])
    o_ref[...] = acc


def _ffn(h, g, w1, w3, w2, *, layer):
    n, d = h.shape
    fdim = w1.shape[2]
    tm = min(TOKEN_TILE, n)
    return pl.pallas_call(
        functools.partial(_ffn_kernel, fdim=fdim),
        out_shape=jax.ShapeDtypeStruct((n, d), F32),
        grid=(n // tm,),
        in_specs=[
            pl.BlockSpec((tm, d), lambda i: (i, 0)),
            _layer_spec(g, layer),
            _layer_spec(w1, layer),
            _layer_spec(w3, layer),
            _layer_spec(w2, layer),
        ],
        out_specs=pl.BlockSpec((tm, d), lambda i: (i, 0)),
        compiler_params=_params("arbitrary"),
        name="dense_swiglu",
    )(h, g, w1, w3, w2)


def _odd_in_kernel(h_ref, g_ref, wt_ref, o_ref, *, nck):
    hn = _rms(h_ref[...], g_ref[...]).astype(BF16)
    zt = lax.dot_general(wt_ref[...], hn, (((1,), (1,)), ((), ())), preferred_element_type=F32)
    for c in range(nck):
        o_ref[c] = zt[:, c * LANES:(c + 1) * LANES]


def _odd_in(h, g, w_in_t, *, layer):
    n, d = h.shape
    cols = w_in_t.shape[1]
    tm = min(LIGHT_TOKEN_TILE, n)
    nck = tm // LANES
    return pl.pallas_call(
        functools.partial(_odd_in_kernel, nck=nck),
        out_shape=jax.ShapeDtypeStruct((n // LANES, cols, LANES), F32),
        grid=(n // tm,),
        in_specs=[
            pl.BlockSpec((tm, d), lambda i: (i, 0)),
            _layer_spec(g, layer),
            _layer_spec(w_in_t, layer),
        ],
        out_specs=pl.BlockSpec((nck, cols, LANES), lambda i: (i, 0, 0)),
        compiler_params=_params("arbitrary"),
        name="odd_in_proj",
    )(h, g, w_in_t)


def _s5_kernel(u_ref, kt_ref, ws_ref, wc_ref, coef_ref, y_ref, m_scr, *, hdim, cpb, n_steps):
    t = T_SSM
    row = lax.broadcasted_iota(jnp.int32, (t, t), 0)
    col = lax.broadcasted_iota(jnp.int32, (t, t), 1)
    causal = col >= row
    def expand_row_block(j):
        r0 = j * t if isinstance(j, int) else pl.multiple_of(j * t, t)
        for i in range(hdim):
            kb = jnp.broadcast_to(kt_ref[0, j, i:i + 1, :], (t, t))
            tz = pltpu.roll(kb, 0, 1, stride=1, stride_axis=0)
            m_scr[pl.ds(r0, t), i * t:(i + 1) * t] = jnp.where(causal, tz, 0.0).astype(BF16)

    _static_loop(hdim, expand_row_block)

    x = jnp.concatenate([u_ref[:, c, :] for c in range(hdim)], axis=1).astype(BF16)
    y = _dot(x, m_scr[...])
    st = _dot(x, ws_ref[0])
    half = st.shape[1] // 2
    rb = lax.broadcasted_iota(jnp.int32, st.shape, 0) & (cpb - 1)
    for k in range(n_steps):
        d = 1 << k
        sh = jnp.where(rb >= d, pltpu.roll(st, d, 0), 0.0)
        sw = pltpu.roll(sh, half, 1)
        st = st + coef_ref[0, k, 0:1, :] * sh + coef_ref[0, k, 1:2, :] * sw
    prev = jnp.where(rb >= 1, pltpu.roll(st, 1, 0), 0.0)
    wc = jnp.concatenate([wc_ref[0, i] for i in range(hdim)], axis=1)
    y = y + _dot(prev.astype(BF16), wc)
    for i in range(hdim):
        y_ref[:, i, :] = y[:, i * t:(i + 1) * t]


def _s5(zt3, kt, ws, wc, coef, *, layer, groups, cpb):
    nc = zt3.shape[0]
    g, hdim = groups, kt.shape[1]
    n_steps = coef.shape[1]
    g0 = layer * groups
    return pl.pallas_call(
        functools.partial(_s5_kernel, hdim=hdim, cpb=cpb, n_steps=n_steps),
        out_shape=jax.ShapeDtypeStruct((nc, g * hdim, T_SSM), F32),
        grid=(g,),
        in_specs=[
            pl.BlockSpec((nc, hdim, T_SSM), lambda i: (0, i, 0)),
            pl.BlockSpec((1,) + kt.shape[1:], lambda i: (g0 + i, 0, 0, 0)),
            pl.BlockSpec((1,) + ws.shape[1:], lambda i: (g0 + i, 0, 0)),
            pl.BlockSpec((1,) + wc.shape[1:], lambda i: (g0 + i, 0, 0, 0)),
            pl.BlockSpec((1,) + coef.shape[1:], lambda i: (g0 + i, 0, 0, 0)),
        ],
        out_specs=pl.BlockSpec((nc, hdim, T_SSM), lambda i: (0, i, 0)),
        scratch_shapes=[pltpu.VMEM((hdim * T_SSM, hdim * T_SSM), BF16)],
        compiler_params=_params("arbitrary"),
        name="s5_chunked",
    )(zt3, kt, ws, wc, coef)


def _s5_tables(lam_re, lam_im, log_dt, b_re, b_im, c_re, c_im, *, cpb):
    flat = lambda v: v.reshape((-1,) + v.shape[2:])
    cat = lambda u, v: jnp.concatenate([u, v], axis=-1)
    lr, li, log_dt = flat(lam_re), flat(lam_im), flat(log_dt)
    b_re, b_im, c_re, c_im = flat(b_re), flat(b_im), flat(c_re), flat(c_im)
    gg, p = lr.shape
    hdim = b_re.shape[2]
    t = T_SSM
    dt = jnp.exp(log_dt)[:, None]
    mag = jnp.exp(lr * dt)
    abar_r = mag * jnp.cos(li * dt)
    abar_i = mag * jnp.sin(li * dt)
    qr, qi = abar_r - 1.0, abar_i
    den = lr * lr + li * li
    fr = ((qr * lr + qi * li) / den)[..., None]
    fi = ((qi * lr - qr * li) / den)[..., None]
    bjr = (fr * b_re - fi * b_im).transpose(0, 2, 1)
    bji = (fr * b_im + fi * b_re).transpose(0, 2, 1)

    def apow(k):
        kk = k[None, :, None]
        m = jnp.exp(kk * (lr * dt)[:, None, :])
        th = kk * (li * dt)[:, None, :]
        return m * jnp.cos(th), m * jnp.sin(th)

    par, pai = apow(jnp.arange(t + 1, dtype=F32))
    pr, pi = par[:, :t], pai[:, :t]
    p1r, p1i = par[:, 1:], pai[:, 1:]
    cbr = c_re[:, None] * bjr[:, :, None, :] - c_im[:, None] * bji[:, :, None, :]
    cbi = c_re[:, None] * bji[:, :, None, :] + c_im[:, None] * bjr[:, :, None, :]
    kt = jnp.einsum('gjic,glc->gjil', cat(cbr, -cbi), cat(pr, pi),
                    precision=lax.Precision.HIGHEST)
    er, ei = apow(jnp.arange(t - 1, -1, -1, dtype=F32))
    ws = (cat(er, er)[:, None] * cat(bjr, bji)[:, :, None, :]
          + cat(-ei, ei)[:, None] * cat(bji, bjr)[:, :, None, :])
    ws = ws.astype(BF16).reshape(gg, hdim * t, 2 * p)
    p1rt = p1r.transpose(0, 2, 1)
    p1it = p1i.transpose(0, 2, 1)
    wc = (cat(c_re, -c_im)[..., None] * jnp.concatenate([p1rt, p1rt], axis=1)[:, None]
          + cat(-c_im, -c_re)[..., None] * jnp.concatenate([p1it, p1it], axis=1)[:, None])
    wc = wc.astype(BF16)
    n_steps = max(1, int(math.log2(cpb)))
    sr, si = apow(float(t) * (2.0 ** jnp.arange(n_steps, dtype=F32)))
    coef = jnp.stack([cat(sr, sr), cat(-si, si)], axis=2)
    return kt, ws, wc, coef


def _odd_tail_kernel(z_ref, y_ref, h_ref, d_ref, gw_ref, gb_ref, lng_ref, sw_ref, sb_ref,
                     wout_ref, g2_ref, rw_ref, o_ref, xn_ref, route_ref,
                     *, nck, sdim, heads, n_experts):
    def cat(ref, r0, r1):
        return jnp.concatenate([ref[c, r0:r1, :] for c in range(nck)], axis=1)

    def rep(ref, r0=None, r1=None):
        v = ref[...] if r0 is None else ref[r0:r1, :]
        return jnp.concatenate([v] * nck, axis=1)

    ut = cat(z_ref, 0, sdim)
    yt = cat(y_ref, 0, sdim)
    gt = jax.nn.gelu(yt + rep(d_ref) * ut)
    glu = _dot(gw_ref[...], gt.astype(BF16)) + rep(gb_ref)
    parts = [gt * jax.nn.sigmoid(glu)]

    hd = sdim // heads
    for hh in range(heads):
        r0 = sdim + hh * hd
        up = jax.nn.gelu(cat(z_ref, r0, r0 + hd))
        vp = jax.nn.gelu(cat(z_ref, sdim + r0, sdim + r0 + hd))
        mu = jnp.mean(vp, axis=0, keepdims=True)
        vc = vp - mu
        var = jnp.mean(vc * vc, axis=0, keepdims=True)
        vn = vc * lax.rsqrt(var + LN_EPS) * rep(lng_ref, hh * hd, (hh + 1) * hd)
        stacked = jnp.concatenate([vn[:, c * LANES:(c + 1) * LANES] for c in range(nck)], axis=0)
        s = _dot(stacked.astype(BF16), sw_ref[hh]) + sb_ref[hh]
        s = jnp.concatenate([s[c * hd:(c + 1) * hd, :] for c in range(nck)], axis=1)
        parts.append(up * s)

    mix = jnp.concatenate(parts, axis=0).T.astype(BF16)
    h1 = h_ref[...] + _dot(mix, wout_ref[...])
    o_ref[...] = h1

    xn = _rms(h1, g2_ref[...])
    xn_ref[...] = xn
    xh = xn.astype(BF16)
    xl = (xn - xh.astype(F32)).astype(BF16)
    hi_terms = _dot(xh, rw_ref[...])
    logits = hi_terms[:, 0:LANES] + (hi_terms[:, LANES:] + _dot(xl, rw_ref[:, 0:LANES]))
    lane = lax.broadcasted_iota(jnp.int32, logits.shape, 1)
    neg = jnp.float32(-jnp.inf)
    lg = jnp.where(lane < n_experts, logits, neg)
    m1 = jnp.max(lg, axis=1, keepdims=True)
    i1 = jnp.min(jnp.where(lg == m1, lane, LANES), axis=1, keepdims=True)
    lg2 = jnp.where(lane == i1, neg, lg)
    m2 = jnp.max(lg2, axis=1, keepdims=True)
    i2 = jnp.min(jnp.where(lg2 == m2, lane, LANES), axis=1, keepdims=True)
    e2 = jnp.exp(m2 - m1)
    w1 = 1.0 / (1.0 + e2)
    w2 = e2 / (1.0 + e2)
    route_ref[...] = (jnp.where(lane == ID_LANE, i1.astype(F32), 0.0)
                      + jnp.where(lane == ID_LANE + 1, i2.astype(F32), 0.0)
                      + jnp.where(lane == GATE_LANE, w1, 0.0)
                      + jnp.where(lane == GATE_LANE + 1, w2, 0.0))


def _odd_tail(zt3, yt3, h, d_b, glu_wt, glu_b_b, lng_b, sgu_wt, sgu_b, w_out, g2, router_hl,
              *, layer, heads, n_experts):
    n, dm = h.shape
    cols = zt3.shape[1]
    sdim = yt3.shape[1]
    tm = min(TOKEN_TILE, n)
    nck = tm // LANES
    kern = functools.partial(_odd_tail_kernel, nck=nck, sdim=sdim, heads=heads, n_experts=n_experts)
    return pl.pallas_call(
        kern,
        out_shape=(jax.ShapeDtypeStruct((n, dm), F32),
                   jax.ShapeDtypeStruct((n, dm), F32),
                   jax.ShapeDtypeStruct((n, LANES), F32)),
        grid=(n // tm,),
        in_specs=[
            pl.BlockSpec((nck, cols, LANES), lambda i: (i, 0, 0)),
            pl.BlockSpec((nck, sdim, LANES), lambda i: (i, 0, 0)),
            pl.BlockSpec((tm, dm), lambda i: (i, 0)),
            _const_spec(d_b.shape),
            _layer_spec(glu_wt, layer),
            _const_spec(glu_b_b.shape),
            _const_spec(lng_b.shape),
            _const_spec(sgu_wt.shape),
            _const_spec(sgu_b.shape),
            _layer_spec(w_out, layer),
            _layer_spec(g2, layer),
            _const_spec(router_hl.shape),
        ],
        out_specs=(pl.BlockSpec((tm, dm), lambda i: (i, 0)),
                   pl.BlockSpec((tm, dm), lambda i: (i, 0)),
                   pl.BlockSpec((tm, LANES), lambda i: (i, 0))),
        compiler_params=_params("arbitrary"),
        name="odd_mixer_tail",
    )(zt3, yt3, h, d_b, glu_wt, glu_b_b, lng_b, sgu_wt, sgu_b, w_out, g2, router_hl)


def _experts_kernel(te_ref, ns_ref, na_ref, x_ref, w1_ref, w3_ref, w2_ref, o_ref, *, sub):
    f = pl.program_id(1)

    @pl.when(f == 0)
    def _():
        o_ref[...] = jnp.zeros(o_ref.shape, F32)

    def swiglu_rows(start, nrows):
        rows = pl.ds(pl.multiple_of(start, sub), nrows)
        x = x_ref[rows, :].astype(BF16)
        a = _dot(x, w1_ref[0].astype(BF16))
        b = _dot(x, w3_ref[0].astype(BF16))
        act = (a * jax.nn.sigmoid(a) * b).astype(BF16)
        o_ref[rows, :] += _dot(act, w2_ref[0].astype(BF16))

    nsub = ns_ref[pl.program_id(0)]
    npair = nsub // 2

    def pair(s, carry):
        swiglu_rows(s * (2 * sub), 2 * sub)
        return carry

    lax.fori_loop(0, npair, pair, 0)

    @pl.when(nsub % 2 == 1)
    def _():
        swiglu_rows(npair * (2 * sub), sub)


def _experts(tile_expert, tile_nsub, n_active, xs, w1, w3, w2, *, layer):
    r, d = xs.shape
    fdim = w1.shape[3]
    te = EXPERT_TILE
    tf = min(EXPERT_F_TILE, fdim)
    n_tiles = r // te
    nf = fdim // tf

    def row_map(i, f, te_ref, ns_ref, na_ref):
        return (jnp.minimum(i, na_ref[0] - 1), 0)

    grid_spec = pltpu.PrefetchScalarGridSpec(
        num_scalar_prefetch=3,
        grid=(n_tiles, nf),
        in_specs=[
            pl.BlockSpec((te, d), row_map),
            pl.BlockSpec((None, 1, d, tf), lambda i, f, te_ref, *_: (layer, te_ref[i], 0, f)),
            pl.BlockSpec((None, 1, d, tf), lambda i, f, te_ref, *_: (layer, te_ref[i], 0, f)),
            pl.BlockSpec((None, 1, tf, d), lambda i, f, te_ref, *_: (layer, te_ref[i], f, 0)),
        ],
        out_specs=pl.BlockSpec((te, d), lambda i, f, *_: (i, 0)),
    )
    return pl.pallas_call(
        functools.partial(_experts_kernel, sub=min(EXPERT_SUB, te)),
        out_shape=jax.ShapeDtypeStruct((r, d), F32),
        grid_spec=grid_spec,
        compiler_params=_params("arbitrary", "arbitrary"),
        name="grouped_experts",
    )(tile_expert, tile_nsub, n_active, xs, w1, w3, w2)


def _dispatch_kernel(te_ref, base_ref, quota_ref, cnt_ref, cb_ref, ct_ref, o_ref, *, te, nb, n):
    i = pl.program_id(0)
    e = te_ref[i]
    lane = lax.broadcasted_iota(jnp.int32, (1, te), 1)
    row = i * te + lane
    rank = base_ref[i] + lane
    blk_end = cb_ref[0][:, LANES - 1:LANES]
    blk = jnp.sum((blk_end <= rank).astype(jnp.int32), axis=0, keepdims=True)
    blk = jnp.minimum(blk, nb - 1)
    onehot = (lax.broadcasted_iota(jnp.int32, (nb, te), 0) == blk).astype(BF16)
    cf = ct_ref[0].astype(F32)
    c_hi = cf.astype(BF16)
    c_lo = (cf - c_hi.astype(F32)).astype(BF16)
    counts_t = _dot(c_hi, onehot) + _dot(c_lo, onehot)
    within = jnp.sum((counts_t <= rank.astype(F32)).astype(jnp.int32), axis=0, keepdims=True)
    filler = row & (n - 1) if n & (n - 1) == 0 else lax.rem(row, n)
    routed = (lane < quota_ref[i]) & (rank < cnt_ref[e])
    o_ref[0] = jnp.where(routed, blk * LANES + within, filler)


def _dispatch_rows(tile_expert, tile_base, tile_quota, counts, cb, ct, *, te, n):
    n_exp, nb, _ = cb.shape
    n_tiles = tile_expert.shape[0]
    assert n < 2 ** 16, "routed counts must split exactly into two bf16 terms"
    grid_spec = pltpu.PrefetchScalarGridSpec(
        num_scalar_prefetch=4,
        grid=(n_tiles,),
        in_specs=[
            pl.BlockSpec((1, nb, LANES), lambda i, te_ref, *_: (te_ref[i], 0, 0)),
            pl.BlockSpec((1, LANES, nb), lambda i, te_ref, *_: (te_ref[i], 0, 0)),
        ],
        out_specs=pl.BlockSpec((1, 1, te), lambda i, *_: (i, 0, 0)),
    )
    return pl.pallas_call(
        functools.partial(_dispatch_kernel, te=te, nb=nb, n=n),
        out_shape=jax.ShapeDtypeStruct((n_tiles, 1, te), jnp.int32),
        grid_spec=grid_spec,
        compiler_params=_params("arbitrary"),
        name="dispatch_rows",
    )(tile_expert, tile_base, tile_quota, counts, cb, ct)


def _moe(h1, xn, route, w1, w3, w2, *, layer):
    n, d = h1.shape
    n_exp = w1.shape[1]
    te = EXPERT_TILE
    sub = min(EXPERT_SUB, te)
    n_tiles = (n * TOP_K + te - 1) // te + n_exp
    nb = n // LANES

    ids = route[:, ID_LANE:ID_LANE + TOP_K].astype(jnp.int32)
    chosen = ids[:, :, None] == jnp.arange(n_exp, dtype=jnp.int32)[None, None, :]
    sel = chosen.any(axis=1)
    csum = jnp.cumsum(sel.astype(jnp.int32), axis=0)
    counts = csum[-1]
    tiles_e = (counts + te - 1) // te
    quota = (counts + jnp.maximum(tiles_e, 1) - 1) // jnp.maximum(tiles_e, 1)
    quota = ((quota + sub - 1) // sub) * sub
    padded = tiles_e * te
    ends = jnp.cumsum(padded)
    offs = ends - padded
    tile_start = jnp.arange(n_tiles, dtype=jnp.int32) * te
    in_use = tile_start < ends[-1]
    n_active = (ends[-1] // te).astype(jnp.int32)
    tile_expert = jnp.sum(tile_start[:, None] >= ends[None, :], axis=1).astype(jnp.int32)
    last_expert = jnp.take(tile_expert, jnp.maximum(n_active - 1, 0))
    tile_expert = jnp.where(in_use, tile_expert, last_expert)
    tile_quota = jnp.where(in_use, quota[tile_expert], 0).astype(jnp.int32)
    tile_base = ((tile_start - offs[tile_expert]) // te * tile_quota).astype(jnp.int32)
    valid = jnp.clip(counts[tile_expert] - tile_base, 0, tile_quota)
    tile_nsub = ((valid + sub - 1) // sub).astype(jnp.int32)

    cb = csum.T.reshape(n_exp, nb, LANES)
    src = _dispatch_rows(tile_expert, tile_base, tile_quota, counts.astype(jnp.int32), cb,
                         cb.transpose(0, 2, 1), te=te, n=n)
    xs = jnp.take(xn, src.reshape(-1), axis=0, mode='clip')
    ys = _experts(tile_expert, tile_nsub, n_active.reshape(1), xs, w1, w3, w2, layer=layer)
    rank = csum - 1
    tile_k = rank // jnp.maximum(quota, 1)[None, :]
    dest = offs[None, :] + tile_k * te + (rank - tile_k * quota[None, :])
    pos = jnp.sum(jnp.where(chosen, dest[:, None, :], 0), axis=2)
    return (h1, jnp.take(ys, pos[:, 0], axis=0, mode='clip'),
            jnp.take(ys, pos[:, 1], axis=0, mode='clip'), route)


def _final_norm_kernel(*refs):
    *res_refs, g_ref, o_ref = refs
    o_ref[...] = _rms(_residual(res_refs), g_ref[...])


def _final_norm(h_parts, g):
    n, d = h_parts[0].shape
    tm = min(LIGHT_TOKEN_TILE, n)
    return pl.pallas_call(
        _final_norm_kernel,
        out_shape=jax.ShapeDtypeStruct((n, d), F32),
        grid=(n // tm,),
        in_specs=_residual_specs(h_parts, tm, lambda i: (i, 0)) + [_const_spec((1, d))],
        out_specs=pl.BlockSpec((tm, d), lambda i: (i, 0)),
        compiler_params=_params("arbitrary"),
        name="final_norm",
    )(*h_parts, g)


def _even_layer(h_parts, p, *, bsz, seq):
    h = _even_mixer(h_parts, p['ev_norm1_g'], p['ev_w_in'], p['ev_conv_w'], p['ev_pool_w'],
                    p['ev_pool_scale'], p['ev_w_out'], layer=0, bsz=bsz, seq=seq)
    return _ffn(h, p['ev_norm2_g'], p['ev_ffn_w1'], p['ev_ffn_w3'], p['ev_ffn_w2'], layer=0)


def _odd_layer(h, p, moe_w, *, moe_layer, seq):
    sgu_w = p['od_sgu_w'][0]
    router_w = p['od_router_w'][0]
    heads, slen, _ = sgu_w.shape
    n_exp = router_w.shape[1]
    assert slen == LANES and seq % T_SSM == 0
    cpb = seq // T_SSM
    assert cpb & (cpb - 1) == 0, "chunks per sequence must be a power of two"

    zt3 = _odd_in(h, p['od_norm1_g'], p['od_w_in_t'], layer=0)
    tables = _s5_tables(p['od_lambda_re'], p['od_lambda_im'], p['od_log_dt'], p['od_b_re'],
                        p['od_b_im'], p['od_c_re'], p['od_c_im'], cpb=cpb)
    yt3 = _s5(zt3, *tables, layer=0, groups=p['od_lambda_re'].shape[1], cpb=cpb)

    lanes_b = lambda v: jnp.broadcast_to(v[:, None], (v.shape[0], LANES))
    cidx = jnp.arange(slen) // CHUNK
    mask = cidx[None, :] <= cidx[:, None]
    sgu_wt = jnp.where(mask[None], sgu_w, 0.0).transpose(0, 2, 1).astype(BF16)
    router_pad = jnp.zeros((router_w.shape[0], LANES), F32).at[:, :n_exp].set(router_w)
    router_hi = router_pad.astype(BF16)
    router_lo = (router_pad - router_hi.astype(F32)).astype(BF16)
    router_hl = jnp.concatenate([router_hi, router_lo], axis=1)
    h1, xn, route = _odd_tail(
        zt3, yt3, h, lanes_b(p['od_d'][0]), p['od_glu_wt'], lanes_b(p['od_glu_b'][0]),
        lanes_b(p['od_sgu_ln_g'][0]), sgu_wt, p['od_sgu_b'][0][:, None, :], p['od_w_out'],
        p['od_norm2_g'], router_hl, layer=0, heads=heads, n_experts=n_exp)
    return _moe(h1, xn, route, *moe_w, layer=moe_layer)


def kernel(x, ev_norm1_g, ev_w_in, ev_conv_w, ev_pool_w, ev_pool_scale, ev_w_out, ev_norm2_g, ev_ffn_w1, ev_ffn_w3, ev_ffn_w2, od_norm1_g, od_w_in, od_lambda_re, od_lambda_im, od_log_dt, od_b_re, od_b_im, od_c_re, od_c_im, od_d, od_glu_w, od_glu_b, od_sgu_ln_g, od_sgu_w, od_sgu_b, od_w_out, od_norm2_g, od_router_w, od_moe_w1, od_moe_w3, od_moe_w2, final_norm_g):
    bsz, seq, d = x.shape
    depth = ev_norm1_g.shape[0] + od_norm1_g.shape[0]
    row = lambda v: v[:, None, :]

    def even_params(i):
        one = lambda v: v[i:i + 1]
        return dict(
            ev_norm1_g=row(one(ev_norm1_g)), ev_w_in=one(ev_w_in).astype(BF16),
            ev_conv_w=one(ev_conv_w), ev_pool_w=one(ev_pool_w).astype(BF16),
            ev_pool_scale=row(one(ev_pool_scale)), ev_w_out=one(ev_w_out).astype(BF16),
            ev_norm2_g=row(one(ev_norm2_g)), ev_ffn_w1=one(ev_ffn_w1).astype(BF16),
            ev_ffn_w3=one(ev_ffn_w3).astype(BF16), ev_ffn_w2=one(ev_ffn_w2).astype(BF16))

    def odd_params(i):
        one = lambda v: v[i:i + 1]
        return dict(
            od_norm1_g=row(one(od_norm1_g)), od_w_in_t=one(od_w_in).transpose(0, 2, 1).astype(BF16),
            od_lambda_re=one(od_lambda_re), od_lambda_im=one(od_lambda_im),
            od_log_dt=one(od_log_dt), od_b_re=one(od_b_re), od_b_im=one(od_b_im),
            od_c_re=one(od_c_re), od_c_im=one(od_c_im), od_d=one(od_d),
            od_glu_wt=one(od_glu_w).transpose(0, 2, 1).astype(BF16), od_glu_b=one(od_glu_b),
            od_sgu_ln_g=one(od_sgu_ln_g), od_sgu_w=one(od_sgu_w), od_sgu_b=one(od_sgu_b),
            od_w_out=one(od_w_out).astype(BF16), od_norm2_g=row(one(od_norm2_g)),
            od_router_w=one(od_router_w))

    moe_w = (od_moe_w1, od_moe_w3, od_moe_w2)
    h_parts = (x.reshape(bsz * seq, d),)
    for layer in range(depth):
        if layer % 2 == 0:
            h_parts = (_even_layer(h_parts, even_params(layer // 2), bsz=bsz, seq=seq),)
        else:
            assert len(h_parts) == 1
            h_parts = _odd_layer(h_parts[0], odd_params(layer // 2), moe_w,
                                 moe_layer=layer // 2, seq=seq)
    return _final_norm(h_parts, final_norm_g[None]).reshape(bsz, seq, d)
```

```python
import functools
import math

import jax
import jax.numpy as jnp
from jax import lax
from jax.experimental import pallas as pl
from jax.experimental.pallas import tpu as pltpu

F32 = jnp.float32
BF16 = jnp.bfloat16

RMS_EPS = 1e-5
LN_EPS = 1e-5
CHUNK = 64
POOL_WINDOWS = (2, 4, 8, 16)
CONV_WIDTH = 3
TOP_K = 2

LANES = 128
V7X_VMEM_BYTES = 64 * 1024 * 1024
VMEM_LIMIT = V7X_VMEM_BYTES * 7 // 8
HALO = 16
T_SSM = LANES
TOKEN_TILE = 512
LIGHT_TOKEN_TILE = 1024
EVEN_ROW_BLOCKS = 2
ID_LANE = LANES - 4
GATE_LANE = LANES - 2
EXPERT_TILE = 2048
EXPERT_SUB = 512
EXPERT_F_TILE = 512
FFN_F_TILE = 1024


def _const_spec(shape):
    nd = len(shape)
    return pl.BlockSpec(shape, lambda *_: (0,) * nd, pipeline_mode=pl.Buffered(1))


def _layer_spec(arr, layer):
    nd = arr.ndim - 1
    return pl.BlockSpec((None,) + arr.shape[1:], lambda *_: (layer,) + (0,) * nd,
                        pipeline_mode=pl.Buffered(1))


def _params(*sem):
    return pltpu.CompilerParams(dimension_semantics=sem, vmem_limit_bytes=VMEM_LIMIT)


def _rms(x, g):
    return x * lax.rsqrt(jnp.mean(x * x, axis=-1, keepdims=True) + RMS_EPS) * g


def _dot(a, b):
    return jnp.dot(a, b, preferred_element_type=F32)


def _static_loop(n, body):
    for j in range(n):
        body(j)


def _residual(refs, rows=slice(None)):
    if len(refs) == 1:
        return refs[0][rows, :]
    h_ref, ya_ref, yb_ref, route_ref = refs
    route = route_ref[rows, :]
    ga = route[:, GATE_LANE:GATE_LANE + 1]
    gb = route[:, GATE_LANE + 1:GATE_LANE + 2]
    return h_ref[rows, :] + ga * ya_ref[rows, :] + gb * yb_ref[rows, :]


def _residual_specs(h_parts, tm, index_map):
    return [pl.BlockSpec((tm, p.shape[1]), index_map) for p in h_parts]


def _even_mixer_kernel(*refs, tm, cdim, nres):
    res_refs = refs[:nres]
    g_ref, win_ref, cw_ref, pw_ref, ps_ref, wout_ref, o_ref, zc_ext, p_ext = refs[nres:]
    j = pl.program_id(1)

    @pl.when(j == 0)
    def _():
        zc_ext[0:HALO, :] = jnp.zeros((HALO, cdim), F32)
        p_ext[0:HALO, :] = jnp.zeros((HALO, cdim), F32)

    rb = tm // EVEN_ROW_BLOCKS
    for q in range(EVEN_ROW_BLOCKS):
        r0 = q * rb
        e0 = HALO + r0
        h = _residual(res_refs, slice(r0, r0 + rb))
        hn = _rms(h, g_ref[...]).astype(BF16)
        z = _dot(hn, win_ref[...])
        gb = z[:, 0:cdim]
        zc = z[:, cdim:2 * cdim] * z[:, 2 * cdim:3 * cdim]
        p = z[:, 3 * cdim:4 * cdim]
        zc_ext[e0:e0 + rb, :] = zc
        p_ext[e0:e0 + rb, :] = p

        y = cw_ref[CONV_WIDTH - 1:CONV_WIDTH, :] * zc
        for k in range(CONV_WIDTH - 1):
            sh = CONV_WIDTH - 1 - k
            y = y + cw_ref[k:k + 1, :] * zc_ext[e0 - sh:e0 - sh + rb, :]
        parts = [(gb * y).astype(BF16)]

        pos = (j * tm + r0 + 1 + lax.broadcasted_iota(jnp.int32, (rb, LANES), 0)).astype(F32)
        for gi, w in enumerate(POOL_WINDOWS):
            c0 = gi * LANES
            pg = p[:, c0:c0 + LANES]
            s = pg
            for k in range(1, w):
                s = s + p_ext[e0 - k:e0 - k + rb, c0:c0 + LANES]
            pooled = s / jnp.minimum(pos, float(w)) - pg
            mixed = _dot(pooled.astype(BF16), pw_ref[gi]) * ps_ref[:, c0:c0 + LANES]
            parts.append(mixed.astype(BF16))

        mix = jnp.concatenate(parts, axis=1)
        o_ref[r0:r0 + rb, :] = h + _dot(mix, wout_ref[...])

    zc_ext[0:HALO, :] = zc_ext[tm:tm + HALO, :]
    p_ext[0:HALO, :] = p_ext[tm:tm + HALO, :]


def _even_mixer(h_parts, g, w_in, conv_w, pool_w, pool_scale, w_out, *, layer, bsz, seq):
    n, d = h_parts[0].shape
    cdim = w_in.shape[2] // 4
    tm = min(TOKEN_TILE, seq)
    nj = seq // tm
    kern = functools.partial(_even_mixer_kernel, tm=tm, cdim=cdim, nres=len(h_parts))
    return pl.pallas_call(
        kern,
        out_shape=jax.ShapeDtypeStruct((n, d), F32),
        grid=(bsz, nj),
        in_specs=_residual_specs(h_parts, tm, lambda b, j: (b * nj + j, 0)) + [
            _layer_spec(g, layer),
            _layer_spec(w_in, layer),
            _layer_spec(conv_w, layer),
            _layer_spec(pool_w, layer),
            _layer_spec(pool_scale, layer),
            _layer_spec(w_out, layer),
        ],
        out_specs=pl.BlockSpec((tm, d), lambda b, j: (b * nj + j, 0)),
        scratch_shapes=[pltpu.VMEM((HALO + tm, cdim), F32), pltpu.VMEM((HALO + tm, cdim), F32)],
        compiler_params=_params("arbitrary", "arbitrary"),
        name="even_mixer",
    )(*h_parts, g, w_in, conv_w, pool_w, pool_scale, w_out)


def _ffn_kernel(h_ref, g_ref, w1_ref, w3_ref, w2_ref, o_ref, *, fdim):
    h = h_ref[...]
    hn = _rms(h, g_ref[...]).astype(BF16)
    acc = h
    for f0 in range(0, fdim, FFN_F_TILE):
        f1 = min(f0 + FFN_F_TILE, fdim)
        a = _dot(hn, w1_ref[:, f0:f1])
        b = _dot(hn, w3_ref[:, f0:f1])
        act = (a * jax.nn.sigmoid(a) * b).astype(BF16)
        acc = acc + _dot(act, w2_ref[f0:f1, :])
    o_ref[...] = acc


def _ffn(h, g, w1, w3, w2, *, layer):
    n, d = h.shape
    fdim = w1.shape[2]
    tm = min(LIGHT_TOKEN_TILE, n)
    return pl.pallas_call(
        functools.partial(_ffn_kernel, fdim=fdim),
        out_shape=jax.ShapeDtypeStruct((n, d), F32),
        grid=(n // tm,),
        in_specs=[
            pl.BlockSpec((tm, d), lambda i: (i, 0)),
            _layer_spec(g, layer),
            _layer_spec(w1, layer),
            _layer_spec(w3, layer),
            _layer_spec(w2, layer),
        ],
        out_specs=pl.BlockSpec((tm, d), lambda i: (i, 0)),
        compiler_params=_params("arbitrary"),
        name="dense_swiglu",
    )(h, g, w1, w3, w2)


def _odd_in_kernel(h_ref, g_ref, wt_ref, o_ref, *, nck):
    hn = _rms(h_ref[...], g_ref[...]).astype(BF16)
    zt = lax.dot_general(wt_ref[...], hn, (((1,), (1,)), ((), ())), preferred_element_type=F32)
    for c in range(nck):
        o_ref[c] = zt[:, c * LANES:(c + 1) * LANES]


def _odd_in(h, g, w_in_t, *, layer):
    n, d = h.shape
    cols = w_in_t.shape[1]
    tm = min(LIGHT_TOKEN_TILE, n)
    nck = tm // LANES
    return pl.pallas_call(
        functools.partial(_odd_in_kernel, nck=nck),
        out_shape=jax.ShapeDtypeStruct((n // LANES, cols, LANES), F32),
        grid=(n // tm,),
        in_specs=[
            pl.BlockSpec((tm, d), lambda i: (i, 0)),
            _layer_spec(g, layer),
            _layer_spec(w_in_t, layer),
        ],
        out_specs=pl.BlockSpec((nck, cols, LANES), lambda i: (i, 0, 0)),
        compiler_params=_params("arbitrary"),
        name="odd_in_proj",
    )(h, g, w_in_t)


def _s5_kernel(u_ref, kt_ref, ws_ref, wc_ref, coef_ref, y_ref, m_scr, *, hdim, cpb, n_steps):
    t = T_SSM
    row = lax.broadcasted_iota(jnp.int32, (t, t), 0)
    col = lax.broadcasted_iota(jnp.int32, (t, t), 1)
    causal = col >= row
    def expand_row_block(j):
        r0 = j * t if isinstance(j, int) else pl.multiple_of(j * t, t)
        for i in range(hdim):
            kb = jnp.broadcast_to(kt_ref[0, j, i:i + 1, :], (t, t))
            tz = pltpu.roll(kb, 0, 1, stride=1, stride_axis=0)
            m_scr[pl.ds(r0, t), i * t:(i + 1) * t] = jnp.where(causal, tz, 0.0).astype(BF16)

    _static_loop(hdim, expand_row_block)

    x = jnp.concatenate([u_ref[:, c, :] for c in range(hdim)], axis=1).astype(BF16)
    y = _dot(x, m_scr[...])
    st = _dot(x, ws_ref[0])
    half = st.shape[1] // 2
    rb = lax.broadcasted_iota(jnp.int32, st.shape, 0) & (cpb - 1)
    for k in range(n_steps):
        d = 1 << k
        sh = jnp.where(rb >= d, pltpu.roll(st, d, 0), 0.0)
        sw = pltpu.roll(sh, half, 1)
        st = st + coef_ref[0, k, 0:1, :] * sh + coef_ref[0, k, 1:2, :] * sw
    prev = jnp.where(rb >= 1, pltpu.roll(st, 1, 0), 0.0)
    wc = jnp.concatenate([wc_ref[0, i] for i in range(hdim)], axis=1)
    y = y + _dot(prev.astype(BF16), wc)
    for i in range(hdim):
        y_ref[:, i, :] = y[:, i * t:(i + 1) * t]


def _s5(zt3, kt, ws, wc, coef, *, layer, groups, cpb):
    nc = zt3.shape[0]
    g, hdim = groups, kt.shape[1]
    n_steps = coef.shape[1]
    g0 = layer * groups
    return pl.pallas_call(
        functools.partial(_s5_kernel, hdim=hdim, cpb=cpb, n_steps=n_steps),
        out_shape=jax.ShapeDtypeStruct((nc, g * hdim, T_SSM), F32),
        grid=(g,),
        in_specs=[
            pl.BlockSpec((nc, hdim, T_SSM), lambda i: (0, i, 0)),
            pl.BlockSpec((1,) + kt.shape[1:], lambda i: (g0 + i, 0, 0, 0)),
            pl.BlockSpec((1,) + ws.shape[1:], lambda i: (g0 + i, 0, 0)),
            pl.BlockSpec((1,) + wc.shape[1:], lambda i: (g0 + i, 0, 0, 0)),
            pl.BlockSpec((1,) + coef.shape[1:], lambda i: (g0 + i, 0, 0, 0)),
        ],
        out_specs=pl.BlockSpec((nc, hdim, T_SSM), lambda i: (0, i, 0)),
        scratch_shapes=[pltpu.VMEM((hdim * T_SSM, hdim * T_SSM), BF16)],
        compiler_params=_params("arbitrary"),
        name="s5_chunked",
    )(zt3, kt, ws, wc, coef)


def _s5_tables(lam_re, lam_im, log_dt, b_re, b_im, c_re, c_im, *, cpb):
    flat = lambda v: v.reshape((-1,) + v.shape[2:])
    cat = lambda u, v: jnp.concatenate([u, v], axis=-1)
    lr, li, log_dt = flat(lam_re), flat(lam_im), flat(log_dt)
    b_re, b_im, c_re, c_im = flat(b_re), flat(b_im), flat(c_re), flat(c_im)
    gg, p = lr.shape
    hdim = b_re.shape[2]
    t = T_SSM
    dt = jnp.exp(log_dt)[:, None]
    mag = jnp.exp(lr * dt)
    abar_r = mag * jnp.cos(li * dt)
    abar_i = mag * jnp.sin(li * dt)
    qr, qi = abar_r - 1.0, abar_i
    den = lr * lr + li * li
    fr = ((qr * lr + qi * li) / den)[..., None]
    fi = ((qi * lr - qr * li) / den)[..., None]
    bjr = (fr * b_re - fi * b_im).transpose(0, 2, 1)
    bji = (fr * b_im + fi * b_re).transpose(0, 2, 1)

    def apow(k):
        kk = k[None, :, None]
        m = jnp.exp(kk * (lr * dt)[:, None, :])
        th = kk * (li * dt)[:, None, :]
        return m * jnp.cos(th), m * jnp.sin(th)

    par, pai = apow(jnp.arange(t + 1, dtype=F32))
    pr, pi = par[:, :t], pai[:, :t]
    p1r, p1i = par[:, 1:], pai[:, 1:]
    cbr = c_re[:, None] * bjr[:, :, None, :] - c_im[:, None] * bji[:, :, None, :]
    cbi = c_re[:, None] * bji[:, :, None, :] + c_im[:, None] * bjr[:, :, None, :]
    kt = jnp.einsum('gjic,glc->gjil', cat(cbr, -cbi), cat(pr, pi),
                    precision=lax.Precision.HIGHEST)
    er, ei = apow(jnp.arange(t - 1, -1, -1, dtype=F32))
    ws = (cat(er, er)[:, None] * cat(bjr, bji)[:, :, None, :]
          + cat(-ei, ei)[:, None] * cat(bji, bjr)[:, :, None, :])
    ws = ws.astype(BF16).reshape(gg, hdim * t, 2 * p)
    p1rt = p1r.transpose(0, 2, 1)
    p1it = p1i.transpose(0, 2, 1)
    wc = (cat(c_re, -c_im)[..., None] * jnp.concatenate([p1rt, p1rt], axis=1)[:, None]
          + cat(-c_im, -c_re)[..., None] * jnp.concatenate([p1it, p1it], axis=1)[:, None])
    wc = wc.astype(BF16)
    n_steps = max(1, int(math.log2(cpb)))
    sr, si = apow(float(t) * (2.0 ** jnp.arange(n_steps, dtype=F32)))
    coef = jnp.stack([cat(sr, sr), cat(-si, si)], axis=2)
    return kt, ws, wc, coef


def _odd_tail_kernel(z_ref, y_ref, h_ref, d_ref, gw_ref, gb_ref, lng_ref, sw_ref, sb_ref,
                     wout_ref, g2_ref, rw_ref, o_ref, xn_ref, route_ref,
                     *, nck, sdim, heads, n_experts):
    def cat(ref, r0, r1):
        return jnp.concatenate([ref[c, r0:r1, :] for c in range(nck)], axis=1)

    def rep(ref, r0=None, r1=None):
        v = ref[...] if r0 is None else ref[r0:r1, :]
        return jnp.concatenate([v] * nck, axis=1)

    ut = cat(z_ref, 0, sdim)
    yt = cat(y_ref, 0, sdim)
    gt = jax.nn.gelu(yt + rep(d_ref) * ut)
    glu = _dot(gw_ref[...], gt.astype(BF16)) + rep(gb_ref)
    parts = [gt * jax.nn.sigmoid(glu)]

    hd = sdim // heads
    for hh in range(heads):
        r0 = sdim + hh * hd
        up = jax.nn.gelu(cat(z_ref, r0, r0 + hd))
        vp = jax.nn.gelu(cat(z_ref, sdim + r0, sdim + r0 + hd))
        mu = jnp.mean(vp, axis=0, keepdims=True)
        vc = vp - mu
        var = jnp.mean(vc * vc, axis=0, keepdims=True)
        vn = vc * lax.rsqrt(var + LN_EPS) * rep(lng_ref, hh * hd, (hh + 1) * hd)
        stacked = jnp.concatenate([vn[:, c * LANES:(c + 1) * LANES] for c in range(nck)], axis=0)
        s = _dot(stacked.astype(BF16), sw_ref[hh]) + sb_ref[hh]
        s = jnp.concatenate([s[c * hd:(c + 1) * hd, :] for c in range(nck)], axis=1)
        parts.append(up * s)

    mix_t = jnp.concatenate(parts, axis=0).astype(BF16)
    h1 = h_ref[...] + lax.dot_general(mix_t, wout_ref[...], (((0,), (0,)), ((), ())),
                                      preferred_element_type=F32)
    o_ref[...] = h1

    xn = _rms(h1, g2_ref[...])
    xn_ref[...] = xn
    xh = xn.astype(BF16)
    xl = (xn - xh.astype(F32)).astype(BF16)
    hi_terms = _dot(xh, rw_ref[...])
    logits = hi_terms[:, 0:LANES] + (hi_terms[:, LANES:] + _dot(xl, rw_ref[:, 0:LANES]))
    lane = lax.broadcasted_iota(jnp.int32, logits.shape, 1)
    neg = jnp.float32(-jnp.inf)
    lg = jnp.where(lane < n_experts, logits, neg)
    m1 = jnp.max(lg, axis=1, keepdims=True)
    i1 = jnp.min(jnp.where(lg == m1, lane, LANES), axis=1, keepdims=True)
    lg2 = jnp.where(lane == i1, neg, lg)
    m2 = jnp.max(lg2, axis=1, keepdims=True)
    i2 = jnp.min(jnp.where(lg2 == m2, lane, LANES), axis=1, keepdims=True)
    e2 = jnp.exp(m2 - m1)
    w1 = 1.0 / (1.0 + e2)
    w2 = e2 / (1.0 + e2)
    route_ref[...] = (jnp.where(lane == ID_LANE, i1.astype(F32), 0.0)
                      + jnp.where(lane == ID_LANE + 1, i2.astype(F32), 0.0)
                      + jnp.where(lane == GATE_LANE, w1, 0.0)
                      + jnp.where(lane == GATE_LANE + 1, w2, 0.0))


def _odd_tail(zt3, yt3, h, d_b, glu_wt, glu_b_b, lng_b, sgu_wt, sgu_b, w_out, g2, router_hl,
              *, layer, heads, n_experts):
    n, dm = h.shape
    cols = zt3.shape[1]
    sdim = yt3.shape[1]
    tm = min(TOKEN_TILE, n)
    nck = tm // LANES
    kern = functools.partial(_odd_tail_kernel, nck=nck, sdim=sdim, heads=heads, n_experts=n_experts)
    return pl.pallas_call(
        kern,
        out_shape=(jax.ShapeDtypeStruct((n, dm), F32),
                   jax.ShapeDtypeStruct((n, dm), F32),
                   jax.ShapeDtypeStruct((n, LANES), F32)),
        grid=(n // tm,),
        in_specs=[
            pl.BlockSpec((nck, cols, LANES), lambda i: (i, 0, 0)),
            pl.BlockSpec((nck, sdim, LANES), lambda i: (i, 0, 0)),
            pl.BlockSpec((tm, dm), lambda i: (i, 0)),
            _const_spec(d_b.shape),
            _layer_spec(glu_wt, layer),
            _const_spec(glu_b_b.shape),
            _const_spec(lng_b.shape),
            _const_spec(sgu_wt.shape),
            _const_spec(sgu_b.shape),
            _layer_spec(w_out, layer),
            _layer_spec(g2, layer),
            _const_spec(router_hl.shape),
        ],
        out_specs=(pl.BlockSpec((tm, dm), lambda i: (i, 0)),
                   pl.BlockSpec((tm, dm), lambda i: (i, 0)),
                   pl.BlockSpec((tm, LANES), lambda i: (i, 0))),
        compiler_params=_params("arbitrary"),
        name="odd_mixer_tail",
    )(zt3, yt3, h, d_b, glu_wt, glu_b_b, lng_b, sgu_wt, sgu_b, w_out, g2, router_hl)


def _experts_kernel(te_ref, ns_ref, na_ref, x_ref, w1_ref, w3_ref, w2_ref, o_ref, *, sub):
    f = pl.program_id(1)

    @pl.when(f == 0)
    def _():
        o_ref[...] = jnp.zeros(o_ref.shape, F32)

    def swiglu_rows(start, nrows):
        rows = pl.ds(pl.multiple_of(start, sub), nrows)
        x = x_ref[rows, :].astype(BF16)
        a = _dot(x, w1_ref[0].astype(BF16))
        b = _dot(x, w3_ref[0].astype(BF16))
        act = (a * jax.nn.sigmoid(a) * b).astype(BF16)
        o_ref[rows, :] += _dot(act, w2_ref[0].astype(BF16))

    nsub = ns_ref[pl.program_id(0)]
    npair = nsub // 2

    def pair(s, carry):
        swiglu_rows(s * (2 * sub), 2 * sub)
        return carry

    lax.fori_loop(0, npair, pair, 0)

    @pl.when(nsub % 2 == 1)
    def _():
        swiglu_rows(npair * (2 * sub), sub)


def _experts(tile_expert, tile_nsub, n_active, xs, w1, w3, w2, *, layer):
    r, d = xs.shape
    fdim = w1.shape[3]
    te = EXPERT_TILE
    tf = min(EXPERT_F_TILE, fdim)
    n_tiles = r // te
    nf = fdim // tf

    def row_map(i, f, te_ref, ns_ref, na_ref):
        return (jnp.minimum(i, na_ref[0] - 1), 0)

    grid_spec = pltpu.PrefetchScalarGridSpec(
        num_scalar_prefetch=3,
        grid=(n_tiles, nf),
        in_specs=[
            pl.BlockSpec((te, d), row_map),
            pl.BlockSpec((None, 1, d, tf), lambda i, f, te_ref, *_: (layer, te_ref[i], 0, f)),
            pl.BlockSpec((None, 1, d, tf), lambda i, f, te_ref, *_: (layer, te_ref[i], 0, f)),
            pl.BlockSpec((None, 1, tf, d), lambda i, f, te_ref, *_: (layer, te_ref[i], f, 0)),
        ],
        out_specs=pl.BlockSpec((te, d), lambda i, f, *_: (i, 0)),
    )
    return pl.pallas_call(
        functools.partial(_experts_kernel, sub=min(EXPERT_SUB, te)),
        out_shape=jax.ShapeDtypeStruct((r, d), F32),
        grid_spec=grid_spec,
        compiler_params=_params("arbitrary", "arbitrary"),
        name="grouped_experts",
    )(tile_expert, tile_nsub, n_active, xs, w1, w3, w2)


def _dispatch_kernel(te_ref, base_ref, quota_ref, cnt_ref, cb_ref, ct_ref, o_ref, *, te, nb, n):
    i = pl.program_id(0)
    e = te_ref[i]
    lane = lax.broadcasted_iota(jnp.int32, (1, te), 1)
    row = i * te + lane
    rank = base_ref[i] + lane
    blk_end = cb_ref[0][:, LANES - 1:LANES]
    blk = jnp.sum((blk_end <= rank).astype(jnp.int32), axis=0, keepdims=True)
    blk = jnp.minimum(blk, nb - 1)
    onehot = (lax.broadcasted_iota(jnp.int32, (nb, te), 0) == blk).astype(BF16)
    cf = ct_ref[0].astype(F32)
    c_hi = cf.astype(BF16)
    c_lo = (cf - c_hi.astype(F32)).astype(BF16)
    counts_t = _dot(c_hi, onehot) + _dot(c_lo, onehot)
    within = jnp.sum((counts_t <= rank.astype(F32)).astype(jnp.int32), axis=0, keepdims=True)
    filler = row & (n - 1) if n & (n - 1) == 0 else lax.rem(row, n)
    routed = (lane < quota_ref[i]) & (rank < cnt_ref[e])
    o_ref[0] = jnp.where(routed, blk * LANES + within, filler)


def _dispatch_rows(tile_expert, tile_base, tile_quota, counts, cb, ct, *, te, n):
    n_exp, nb, _ = cb.shape
    n_tiles = tile_expert.shape[0]
    assert n < 2 ** 16, "routed counts must split exactly into two bf16 terms"
    grid_spec = pltpu.PrefetchScalarGridSpec(
        num_scalar_prefetch=4,
        grid=(n_tiles,),
        in_specs=[
            pl.BlockSpec((1, nb, LANES), lambda i, te_ref, *_: (te_ref[i], 0, 0)),
            pl.BlockSpec((1, LANES, nb), lambda i, te_ref, *_: (te_ref[i], 0, 0)),
        ],
        out_specs=pl.BlockSpec((1, 1, te), lambda i, *_: (i, 0, 0)),
    )
    return pl.pallas_call(
        functools.partial(_dispatch_kernel, te=te, nb=nb, n=n),
        out_shape=jax.ShapeDtypeStruct((n_tiles, 1, te), jnp.int32),
        grid_spec=grid_spec,
        compiler_params=_params("arbitrary"),
        name="dispatch_rows",
    )(tile_expert, tile_base, tile_quota, counts, cb, ct)


def _moe(h1, xn, route, w1, w3, w2, *, layer):
    n, d = h1.shape
    n_exp = w1.shape[1]
    te = EXPERT_TILE
    sub = min(EXPERT_SUB, te)
    n_tiles = (n * TOP_K + te - 1) // te + n_exp
    nb = n // LANES

    ids = route[:, ID_LANE:ID_LANE + TOP_K].astype(jnp.int32)
    chosen = ids[:, :, None] == jnp.arange(n_exp, dtype=jnp.int32)[None, None, :]
    sel = chosen.any(axis=1)
    csum = jnp.cumsum(sel.astype(jnp.int32), axis=0)
    counts = csum[-1]
    tiles_e = (counts + te - 1) // te
    quota = (counts + jnp.maximum(tiles_e, 1) - 1) // jnp.maximum(tiles_e, 1)
    quota = ((quota + sub - 1) // sub) * sub
    padded = tiles_e * te
    ends = jnp.cumsum(padded)
    offs = ends - padded
    tile_start = jnp.arange(n_tiles, dtype=jnp.int32) * te
    in_use = tile_start < ends[-1]
    n_active = (ends[-1] // te).astype(jnp.int32)
    tile_expert = jnp.sum(tile_start[:, None] >= ends[None, :], axis=1).astype(jnp.int32)
    last_expert = jnp.take(tile_expert, jnp.maximum(n_active - 1, 0))
    tile_expert = jnp.where(in_use, tile_expert, last_expert)
    tile_quota = jnp.where(in_use, quota[tile_expert], 0).astype(jnp.int32)
    tile_base = ((tile_start - offs[tile_expert]) // te * tile_quota).astype(jnp.int32)
    valid = jnp.clip(counts[tile_expert] - tile_base, 0, tile_quota)
    tile_nsub = ((valid + sub - 1) // sub).astype(jnp.int32)

    cb = csum.T.reshape(n_exp, nb, LANES)
    src = _dispatch_rows(tile_expert, tile_base, tile_quota, counts.astype(jnp.int32), cb,
                         cb.transpose(0, 2, 1), te=te, n=n)
    xs = jnp.take(xn, src.reshape(-1), axis=0, mode='clip')
    ys = _experts(tile_expert, tile_nsub, n_active.reshape(1), xs, w1, w3, w2, layer=layer)
    rank = csum - 1
    tile_k = rank // jnp.maximum(quota, 1)[None, :]
    dest = offs[None, :] + tile_k * te + (rank - tile_k * quota[None, :])
    pos = jnp.sum(jnp.where(chosen, dest[:, None, :], 0), axis=2)
    return (h1, jnp.take(ys, pos[:, 0], axis=0, mode='clip'),
            jnp.take(ys, pos[:, 1], axis=0, mode='clip'), route)


def _final_norm_kernel(*refs):
    *res_refs, g_ref, o_ref = refs
    o_ref[...] = _rms(_residual(res_refs), g_ref[...])


def _final_norm(h_parts, g):
    n, d = h_parts[0].shape
    tm = min(LIGHT_TOKEN_TILE, n)
    return pl.pallas_call(
        _final_norm_kernel,
        out_shape=jax.ShapeDtypeStruct((n, d), F32),
        grid=(n // tm,),
        in_specs=_residual_specs(h_parts, tm, lambda i: (i, 0)) + [_const_spec((1, d))],
        out_specs=pl.BlockSpec((tm, d), lambda i: (i, 0)),
        compiler_params=_params("arbitrary"),
        name="final_norm",
    )(*h_parts, g)


def _even_layer(h_parts, p, i, *, bsz, seq):
    h = _even_mixer(h_parts, p['ev_norm1_g'], p['ev_w_in'], p['ev_conv_w'], p['ev_pool_w'],
                    p['ev_pool_scale'], p['ev_w_out'], layer=i, bsz=bsz, seq=seq)
    return _ffn(h, p['ev_norm2_g'], p['ev_ffn_w1'], p['ev_ffn_w3'], p['ev_ffn_w2'], layer=i)


def _odd_layer(h, p, i, *, seq):
    sgu_w = p['od_sgu_w'][i]
    router_w = p['od_router_w'][i]
    heads, slen, _ = sgu_w.shape
    n_exp = router_w.shape[1]
    assert slen == LANES and seq % T_SSM == 0
    cpb = seq // T_SSM
    assert cpb & (cpb - 1) == 0, "chunks per sequence must be a power of two"

    zt3 = _odd_in(h, p['od_norm1_g'], p['od_w_in_t'], layer=i)
    one = lambda name: p[name][i:i + 1]
    tables = _s5_tables(one('od_lambda_re'), one('od_lambda_im'), one('od_log_dt'), one('od_b_re'),
                        one('od_b_im'), one('od_c_re'), one('od_c_im'), cpb=cpb)
    yt3 = _s5(zt3, *tables, layer=0, groups=p['od_lambda_re'].shape[1], cpb=cpb)

    lanes_b = lambda v: jnp.broadcast_to(v[:, None], (v.shape[0], LANES))
    cidx = jnp.arange(slen) // CHUNK
    mask = cidx[None, :] <= cidx[:, None]
    sgu_wt = jnp.where(mask[None], sgu_w, 0.0).transpose(0, 2, 1).astype(BF16)
    router_pad = jnp.zeros((router_w.shape[0], LANES), F32).at[:, :n_exp].set(router_w)
    router_hi = router_pad.astype(BF16)
    router_lo = (router_pad - router_hi.astype(F32)).astype(BF16)
    router_hl = jnp.concatenate([router_hi, router_lo], axis=1)
    h1, xn, route = _odd_tail(
        zt3, yt3, h, lanes_b(p['od_d'][i]), p['od_glu_wt'], lanes_b(p['od_glu_b'][i]),
        lanes_b(p['od_sgu_ln_g'][i]), sgu_wt, p['od_sgu_b'][i][:, None, :], p['od_w_out'],
        p['od_norm2_g'], router_hl, layer=i, heads=heads, n_experts=n_exp)
    return _moe(h1, xn, route, p['od_moe_w1'], p['od_moe_w3'], p['od_moe_w2'], layer=i)


def kernel(x, ev_norm1_g, ev_w_in, ev_conv_w, ev_pool_w, ev_pool_scale, ev_w_out, ev_norm2_g, ev_ffn_w1, ev_ffn_w3, ev_ffn_w2, od_norm1_g, od_w_in, od_lambda_re, od_lambda_im, od_log_dt, od_b_re, od_b_im, od_c_re, od_c_im, od_d, od_glu_w, od_glu_b, od_sgu_ln_g, od_sgu_w, od_sgu_b, od_w_out, od_norm2_g, od_router_w, od_moe_w1, od_moe_w3, od_moe_w2, final_norm_g):
    bsz, seq, d = x.shape
    depth = ev_norm1_g.shape[0] + od_norm1_g.shape[0]
    row = lambda v: v[:, None, :]
    p = dict(
        ev_norm1_g=row(ev_norm1_g), ev_w_in=ev_w_in.astype(BF16), ev_conv_w=ev_conv_w,
        ev_pool_w=ev_pool_w.astype(BF16), ev_pool_scale=row(ev_pool_scale),
        ev_w_out=ev_w_out.astype(BF16), ev_norm2_g=row(ev_norm2_g),
        ev_ffn_w1=ev_ffn_w1.astype(BF16), ev_ffn_w3=ev_ffn_w3.astype(BF16),
        ev_ffn_w2=ev_ffn_w2.astype(BF16),
        od_norm1_g=row(od_norm1_g), od_w_in_t=od_w_in.transpose(0, 2, 1).astype(BF16),
        od_lambda_re=od_lambda_re, od_lambda_im=od_lambda_im, od_log_dt=od_log_dt,
        od_b_re=od_b_re, od_b_im=od_b_im, od_c_re=od_c_re, od_c_im=od_c_im, od_d=od_d,
        od_glu_wt=od_glu_w.transpose(0, 2, 1).astype(BF16), od_glu_b=od_glu_b,
        od_sgu_ln_g=od_sgu_ln_g, od_sgu_w=od_sgu_w, od_sgu_b=od_sgu_b,
        od_w_out=od_w_out.astype(BF16), od_norm2_g=row(od_norm2_g), od_router_w=od_router_w,
        od_moe_w1=od_moe_w1, od_moe_w3=od_moe_w3, od_moe_w2=od_moe_w2,
    )
    h_parts = (x.reshape(bsz * seq, d),)
    for layer in range(depth):
        if layer % 2 == 0:
            h_parts = (_even_layer(h_parts, p, layer // 2, bsz=bsz, seq=seq),)
        else:
            assert len(h_parts) == 1
            h_parts = _odd_layer(h_parts[0], p, layer // 2, seq=seq)
    return _final_norm(h_parts, final_norm_g[None]).reshape(bsz, seq, d)
```
